```python
import jax, jax.numpy as jnp
from jax import lax
import numpy as np

D_MODEL = 1024
BATCH = 8
SEQ = 4096
DEPTH = 4

N_MIXERS = 2
N_GLA = (DEPTH + 1) // 2
N_POOL = DEPTH // 2
EPS = 1e-6
D_FF = 2816
GLA_HEADS = 4
GLA_KD = D_MODEL // 2
GLA_VD = D_MODEL
GLA_DK = GLA_KD // GLA_HEADS
GLA_DV = GLA_VD // GLA_HEADS
GATE_RANK = 16
GATE_TAU = 16.0
CHUNK = 64
POOL_WINDOWS = (2, 4, 8, 16)
POOL_GROUPS = len(POOL_WINDOWS)
POOL_GC = D_MODEL // POOL_GROUPS
MEM_LEN = 256
X_HEADS = 4
X_DH = D_MODEL // X_HEADS

kernel_name = "hybrid_gla_pool_macaron_memxattn"


def rmsnorm(x, g):
    xf = x.astype(jnp.float32)
    y = xf * lax.rsqrt(jnp.mean(xf * xf, axis=-1, keepdims=True) + EPS)
    return (y * g.astype(jnp.float32)).astype(x.dtype)


def swiglu_ffn(h, w_gu, w_down):
    g, u = jnp.split(h @ w_gu, 2, axis=-1)
    return (jax.nn.silu(g) * u) @ w_down


def gla_mixer(h, w_in, w_g1, w_g2, b_g, g_onorm, w_out):
    B, S, _ = h.shape
    N = S // CHUNK
    f32 = jnp.float32
    proj = h @ w_in
    q, k, v, r = jnp.split(proj, [GLA_KD, 2 * GLA_KD, 2 * GLA_KD + GLA_VD], axis=-1)
    glog = jax.nn.log_sigmoid(((h @ w_g1) @ w_g2 + b_g).astype(f32)) / GATE_TAU

    def heads(t, dh):
        return t.reshape(B, N, CHUNK, GLA_HEADS, dh).transpose(0, 3, 1, 2, 4).astype(f32)

    q = heads(q, GLA_DK) * (GLA_DK ** -0.5)
    k = heads(k, GLA_DK)
    v = heads(v, GLA_DV)
    b = jnp.cumsum(heads(glog, GLA_DK), axis=3)
    b_last = b[:, :, :, -1:, :]
    b_ref = b[:, :, :, CHUNK // 2 - 1:CHUNK // 2, :]

    A = jnp.einsum('bhnik,bhnjk->bhnij', q * jnp.exp(b - b_ref), k * jnp.exp(b_ref - b))
    causal = jnp.tril(jnp.ones((CHUNK, CHUNK), dtype=bool))
    A = jnp.where(causal, A, 0.0)
    o_intra = jnp.einsum('bhnij,bhnjv->bhniv', A, v)

    q_in = q * jnp.exp(b)
    k_st = k * jnp.exp(b_last - b)
    decay = jnp.exp(b_last[:, :, :, 0, :])

    def step(state, xs):
        qn, kn, vn, dn = xs
        o = jnp.einsum('bhik,bhkv->bhiv', qn, state)
        state = dn[..., None] * state + jnp.einsum('bhjk,bhjv->bhkv', kn, vn)
        return state, o

    xs = tuple(jnp.moveaxis(t, 2, 0) for t in (q_in, k_st, v, decay))
    s0 = jnp.zeros((B, GLA_HEADS, GLA_DK, GLA_DV), f32)
    _, o_inter = lax.scan(step, s0, xs)
    o = o_intra + jnp.moveaxis(o_inter, 0, 2)
    o = o.transpose(0, 2, 3, 1, 4).reshape(B, S, GLA_HEADS, GLA_DV)
    o = rmsnorm(o, g_onorm).reshape(B, S, GLA_VD)
    o = (o * jax.nn.silu(r.astype(f32))).astype(h.dtype)
    return o @ w_out


def causal_window_mean(u, w):
    S = u.shape[1]
    cs = jnp.cumsum(u.astype(jnp.float32), axis=1)
    cs = jnp.pad(cs, ((0, 0), (1, 0), (0, 0)))
    upper = cs[:, 1:]
    lower = jnp.pad(cs, ((0, 0), (w - 1, 0), (0, 0)))[:, :S]
    count = jnp.minimum(jnp.arange(1, S + 1), w).astype(jnp.float32)
    return (upper - lower) / count[None, :, None]


def pool_mixer(h, w_grp, b_grp, scale):
    B, S, _ = h.shape
    groups = []
    for gi, w in enumerate(POOL_WINDOWS):
        u = h[..., gi * POOL_GC:(gi + 1) * POOL_GC]
        groups.append(causal_window_mean(u, w) - u.astype(jnp.float32))
    p = jnp.stack(groups, axis=2).astype(h.dtype)
    y = jnp.einsum('bsgc,gcd->bsgd', p, w_grp) + b_grp
    return y.reshape(B, S, D_MODEL) * scale


def mem_cross_attn(h, mem_n, w_q, w_kv, w_o):
    B, S, _ = h.shape
    q = (h @ w_q).reshape(B, S, X_HEADS, X_DH)
    k, v = jnp.split(mem_n @ w_kv, 2, axis=-1)
    k = k.reshape(B, MEM_LEN, X_HEADS, X_DH)
    v = v.reshape(B, MEM_LEN, X_HEADS, X_DH)
    s = jnp.einsum('bshd,bmhd->bhsm', q, k).astype(jnp.float32) * (X_DH ** -0.5)
    p = jax.nn.softmax(s, axis=-1).astype(h.dtype)
    o = jnp.einsum('bhsm,bmhd->bshd', p, v).reshape(B, S, D_MODEL)
    return o @ w_o


def setup_inputs(seed: int = 0) -> dict:
    key = jax.random.key(seed)
    ks = iter(jax.random.split(key, 40))

    def nrm(shape, fan_in):
        return jax.random.normal(next(ks), shape, jnp.float32) * (fan_in ** -0.5)

    def gain(shape):
        return 1.0 + 0.02 * jax.random.normal(next(ks), shape, jnp.float32)

    def small(shape):
        return 0.01 * jax.random.normal(next(ks), shape, jnp.float32)

    D = D_MODEL
    return {
        "x": jax.random.normal(next(ks), (BATCH, SEQ, D), jnp.float32),
        "mem": jax.random.normal(next(ks), (BATCH, MEM_LEN, D), jnp.float32),
        "n_ffn1": gain((DEPTH, D)),
        "ffn1_w_gu": nrm((DEPTH, D, 2 * D_FF), D),
        "ffn1_w_down": nrm((DEPTH, D_FF, D), D_FF),
        "n_mix": gain((DEPTH, D)),
        "gla_w_in": nrm((N_GLA, D, 2 * GLA_KD + 2 * GLA_VD), D),
        "gla_w_g1": nrm((N_GLA, D, GATE_RANK), D),
        "gla_w_g2": nrm((N_GLA, GATE_RANK, GLA_KD), GATE_RANK),
        "gla_b_g": small((N_GLA, GLA_KD)),
        "gla_onorm": gain((N_GLA, GLA_DV)),
        "gla_w_out": nrm((N_GLA, GLA_VD, D), GLA_VD),
        "pool_w": nrm((N_POOL, POOL_GROUPS, POOL_GC, POOL_GC), POOL_GC),
        "pool_b": small((N_POOL, POOL_GROUPS, POOL_GC)),
        "pool_scale": gain((N_POOL, D)),
        "n_xattn": gain((DEPTH, D)),
        "n_mem": gain((DEPTH, D)),
        "x_w_q": nrm((DEPTH, D, D), D),
        "x_w_kv": nrm((DEPTH, D, 2 * D), D),
        "x_w_o": nrm((DEPTH, D, D), D),
        "n_ffn2": gain((DEPTH, D)),
        "ffn2_w_gu": nrm((DEPTH, D, 2 * D_FF), D),
        "ffn2_w_down": nrm((DEPTH, D_FF, D), D_FF),
        "n_final": gain((D,)),
    }


def reference(x, mem, n_ffn1, ffn1_w_gu, ffn1_w_down, n_mix, gla_w_in, gla_w_g1,
              gla_w_g2, gla_b_g, gla_onorm, gla_w_out, pool_w, pool_b, pool_scale,
              n_xattn, n_mem, x_w_q, x_w_kv, x_w_o, n_ffn2, ffn2_w_gu, ffn2_w_down,
              n_final):
    for i in range(DEPTH):
        x = x + 0.5 * swiglu_ffn(rmsnorm(x, n_ffn1[i]), ffn1_w_gu[i], ffn1_w_down[i])
        h = rmsnorm(x, n_mix[i])
        j = i // N_MIXERS
        if i % N_MIXERS == 0:
            mix = gla_mixer(h, gla_w_in[j], gla_w_g1[j], gla_w_g2[j], gla_b_g[j],
                            gla_onorm[j], gla_w_out[j])
        else:
            mix = pool_mixer(h, pool_w[j], pool_b[j], pool_scale[j])
        x = x + mix
        x = x + mem_cross_attn(rmsnorm(x, n_xattn[i]), rmsnorm(mem, n_mem[i]),
                               x_w_q[i], x_w_kv[i], x_w_o[i])
        x = x + 0.5 * swiglu_ffn(rmsnorm(x, n_ffn2[i]), ffn2_w_gu[i], ffn2_w_down[i])
    return rmsnorm(x, n_final)
```

```python
import functools

import jax
import jax.numpy as jnp
from jax import lax
from jax.experimental import pallas as pl
from jax.experimental.pallas import tpu as pltpu

F32 = jnp.float32
BF16 = jnp.bfloat16

D_MODEL = 1024
DEPTH = 4
EPS = 1e-6
D_FF = 2816
GLA_HEADS = 4
GLA_KD = 512
GLA_VD = 1024
GLA_DK = 128
GLA_DV = 256
GATE_RANK = 16
GATE_TAU = 16.0
CHUNK = 64
POOL_WINDOWS = (2, 4, 8, 16)
POOL_GC = 256
POOL_HALO = 16
MEM_LEN = 256
X_HEADS = 4
X_DH = 256

MXU_TILE = 256
FFN_CHUNK = MXU_TILE
FFN_TM = 512
MIX_TM = 512
VMEM_LIMIT = 56 * 1024 * 1024


def _rms(x, g):
    ms = jnp.mean(x * x, axis=-1, keepdims=True)
    return x * lax.rsqrt(ms + EPS) * g


def _resident(shape):
    nd = len(shape)
    return pl.BlockSpec(shape, lambda *_: (0,) * nd, pipeline_mode=pl.Buffered(1))


def _dot(a, b):
    return jnp.dot(a, b, preferred_element_type=F32)


def _dot_nt(a, b):
    return lax.dot_general(a, b, (((1,), (1,)), ((), ())), preferred_element_type=F32)


def _ffn_kernel(x_ref, g_ref, wg_ref, wu_ref, wd_ref, *rest, n_chunks, final_norm):
    if final_norm:
        gf_ref, o_ref = rest
    else:
        (o_ref,) = rest
    x = x_ref[...]
    h = _rms(x, g_ref[...]).astype(BF16)
    acc = None
    for c in range(n_chunks):
        g = _dot(h, wg_ref[c])
        u = _dot(h, wu_ref[c])
        a = (g * jax.nn.sigmoid(g) * u).astype(BF16)
        d = _dot(a, wd_ref[c])
        acc = d if acc is None else acc + d
    y = x + 0.5 * acc
    if final_norm:
        y = _rms(y, gf_ref[...])
    o_ref[...] = y


def _ffn(x2d, g, wg, wu, wd, g_final=None):
    t, d = x2d.shape
    n_chunks = wg.shape[0]
    final_norm = g_final is not None
    in_specs = [
        pl.BlockSpec((FFN_TM, d), lambda i: (i, 0)),
        _resident((1, d)),
        _resident(wg.shape),
        _resident(wu.shape),
        _resident(wd.shape),
    ]
    args = [x2d, g.reshape(1, d), wg, wu, wd]
    if final_norm:
        in_specs.append(_resident((1, d)))
        args.append(g_final.reshape(1, d))
    return pl.pallas_call(
        functools.partial(_ffn_kernel, n_chunks=n_chunks, final_norm=final_norm),
        grid=(t // FFN_TM,),
        in_specs=in_specs,
        out_specs=pl.BlockSpec((FFN_TM, d), lambda i: (i, 0)),
        out_shape=jax.ShapeDtypeStruct((t, d), F32),
        compiler_params=pltpu.CompilerParams(
            dimension_semantics=("parallel",), vmem_limit_bytes=VMEM_LIMIT),
        name="ffn",
    )(*args)


def _kv_kernel(mem_ref, g_ref, w_ref, o_ref):
    mn = _rms(mem_ref[0], g_ref[0]).astype(BF16)
    o_ref[0, 0] = _dot(mn, w_ref[0]).astype(BF16)


def _mem_kv(mem, n_mem, w_kv):
    b, m, d = mem.shape
    depth = w_kv.shape[0]
    return pl.pallas_call(
        _kv_kernel,
        grid=(depth, b),
        in_specs=[
            pl.BlockSpec((1, m, d), lambda l, i: (i, 0, 0)),
            pl.BlockSpec((1, 1, d), lambda l, i: (l, 0, 0)),
            pl.BlockSpec((1, d, 2 * d), lambda l, i: (l, 0, 0)),
        ],
        out_specs=pl.BlockSpec((1, 1, m, 2 * d), lambda l, i: (l, i, 0, 0)),
        out_shape=jax.ShapeDtypeStruct((depth, b, m, 2 * d), BF16),
        compiler_params=pltpu.CompilerParams(
            dimension_semantics=("arbitrary", "arbitrary"), vmem_limit_bytes=VMEM_LIMIT),
        name="mem_kv",
    )(mem, n_mem.reshape(depth, 1, d), w_kv)


def _pool_mix(x, xp_ref, gmix_ref, pw_ref, pb_ref, ps_ref, ext_ref, tm):
    i = pl.program_id(1)
    gmix = gmix_ref[...]
    hp = _rms(x, gmix)
    hprev = _rms(xp_ref[0], gmix)
    ext_ref[0:POOL_HALO, :] = jnp.where(i > 0, hprev, 0.0)
    ext_ref[POOL_HALO:, :] = hp
    pos = i * tm + lax.broadcasted_iota(jnp.int32, (tm, 1), 0)
    ys = []
    for gi, w in enumerate(POOL_WINDOWS):
        lanes = slice(gi * POOL_GC, (gi + 1) * POOL_GC)
        u = hp[:, lanes]
        win = u
        for k in range(1, w):
            win = win + ext_ref[POOL_HALO - k:POOL_HALO - k + tm, lanes]
        cnt = jnp.minimum(pos + 1, w).astype(F32)
        p = win / cnt - u
        ys.append(_dot(p.astype(BF16), pw_ref[gi]) + pb_ref[gi:gi + 1, :])
    y = jnp.concatenate(ys, axis=1) * ps_ref[...]
    return x + y


def _xattn_kernel(*refs, pool, tm):
    if pool:
        (x_ref, xp_ref, gmix_ref, pw_ref, pb_ref, ps_ref,
         gx_ref, wq_ref, kv_ref, wo_ref, o_ref, ext_ref) = refs
    else:
        x_ref, gx_ref, wq_ref, kv_ref, wo_ref, o_ref = refs
    x = x_ref[0]
    if pool:
        x = _pool_mix(x, xp_ref, gmix_ref, pw_ref, pb_ref, ps_ref, ext_ref, tm)
    hn = _rms(x, gx_ref[...]).astype(BF16)
    q = (_dot(hn, wq_ref[...]) * (X_DH ** -0.5)).astype(BF16)
    outs = []
    for h in range(X_HEADS):
        lanes = slice(h * X_DH, (h + 1) * X_DH)
        kh = kv_ref[0, 0, :, h * X_DH:(h + 1) * X_DH]
        vh = kv_ref[0, 0, :, D_MODEL + h * X_DH:D_MODEL + (h + 1) * X_DH]
        s = _dot_nt(q[:, lanes], kh)
        e = jnp.exp(s - jnp.max(s, axis=-1, keepdims=True))
        p = e / jnp.sum(e, axis=-1, keepdims=True)
        outs.append(_dot(p.astype(BF16), vh).astype(BF16))
    o = jnp.concatenate(outs, axis=1)
    o_ref[0] = x + _dot(o, wo_ref[...])


def _xattn(x, layer, kv, gx, wq, wo, pool_args=None):
    b, s, d = x.shape
    tm = MIX_TM
    pool = pool_args is not None
    x_spec = pl.BlockSpec((1, tm, d), lambda bi, i: (bi, i, 0))
    in_specs = [x_spec]
    args = [x]
    scratch = []
    if pool:
        gmix, pw, pb, ps = pool_args
        halo_blocks = tm // POOL_HALO
        in_specs += [
            pl.BlockSpec((1, POOL_HALO, d),
                         lambda bi, i: (bi, jnp.maximum(i * halo_blocks - 1, 0), 0)),
            _resident((1, d)),
            _resident(pw.shape),
            _resident(pb.shape),
            _resident((1, d)),
        ]
        args += [x, gmix.reshape(1, d), pw, pb, ps.reshape(1, d)]
        scratch = [pltpu.VMEM((tm + POOL_HALO, d), F32)]
    in_specs += [
        _resident((1, d)),
        _resident(wq.shape),
        pl.BlockSpec((1, 1, MEM_LEN, 2 * d), lambda bi, i: (layer, bi, 0, 0)),
        _resident(wo.shape),
    ]
    args += [gx.reshape(1, d), wq, kv, wo]
    return pl.pallas_call(
        functools.partial(_xattn_kernel, pool=pool, tm=tm),
        grid=(b, s // tm),
        in_specs=in_specs,
        out_specs=x_spec,
        out_shape=jax.ShapeDtypeStruct((b, s, d), F32),
        scratch_shapes=scratch,
        compiler_params=pltpu.CompilerParams(
            dimension_semantics=("arbitrary", "arbitrary"), vmem_limit_bytes=VMEM_LIMIT),
        name="pool_xattn" if pool else "xattn",
    )(*args)


def _gla_kernel(x_ref, gmix_ref, win_ref, wg1_ref, wg2_ref, bg_ref, gon_ref, wout_ref,
                o_ref, state_ref, oacc_ref, *, tm):
    @pl.when(pl.program_id(1) == 0)
    def _():
        state_ref[...] = jnp.zeros_like(state_ref)

    x = x_ref[0]
    h = _rms(x, gmix_ref[...]).astype(BF16)
    proj = _dot(h, win_ref[...])
    z = _dot(_dot(h, wg1_ref[...]).astype(BF16), wg2_ref[...]) + bg_ref[...]
    glog = jax.nn.log_sigmoid(z) * (1.0 / GATE_TAU)
    g_hi = glog.astype(BF16)
    g_lo = (glog - g_hi.astype(F32)).astype(BF16)
    row = lax.broadcasted_iota(jnp.int32, (CHUNK, CHUNK), 0)
    col = lax.broadcasted_iota(jnp.int32, (CHUNK, CHUNK), 1)
    causal = row >= col
    tril = causal.astype(BF16)

    for c in range(tm // CHUNK):
        rows = slice(c * CHUNK, (c + 1) * CHUNK)
        b = _dot(tril, g_hi[rows]) + _dot(tril, g_lo[rows])
        b_ref = b[CHUNK // 2 - 1:CHUNK // 2]
        b_last = b[CHUNK - 1:CHUNK]
        q = proj[rows, 0:GLA_KD] * (GLA_DK ** -0.5)
        k = proj[rows, GLA_KD:2 * GLA_KD]
        v = proj[rows, 2 * GLA_KD:2 * GLA_KD + GLA_VD].astype(BF16)
        qa = (q * jnp.exp(b - b_ref)).astype(BF16)
        kb = (k * jnp.exp(b_ref - b)).astype(BF16)
        qi = (q * jnp.exp(b)).astype(BF16)
        kst_t = (k * jnp.exp(b_last - b)).T.astype(BF16)
        decay = jnp.exp(b[CHUNK - 8:CHUNK].T[:, 7:8])
        for hh in range(GLA_HEADS):
            kl = slice(hh * GLA_DK, (hh + 1) * GLA_DK)
            vl = slice(hh * GLA_DV, (hh + 1) * GLA_DV)
            a = jnp.where(causal, _dot_nt(qa[:, kl], kb[:, kl]), 0.0)
            st = state_ref[hh]
            oacc_ref[rows, vl] = _dot(a.astype(BF16), v[:, vl]) + _dot(qi[:, kl], st.astype(BF16))
            state_ref[hh] = decay[kl] * st + _dot(kst_t[kl], v[:, vl])

    r = proj[:, 2 * GLA_KD + GLA_VD:]
    gate = r * jax.nn.sigmoid(r)
    gon = gon_ref[...]
    outs = []
    for hh in range(GLA_HEADS):
        vl = slice(hh * GLA_DV, (hh + 1) * GLA_DV)
        outs.append((_rms(oacc_ref[:, vl], gon) * gate[:, vl]).astype(BF16))
    o_ref[0] = x + _dot(jnp.concatenate(outs, axis=1), wout_ref[...])


def _gla(x, gmix, w_in, w_g1, w_g2, b_g, g_on, w_out):
    b, s, d = x.shape
    tm = MIX_TM
    x_spec = pl.BlockSpec((1, tm, d), lambda bi, i: (bi, i, 0))
    return pl.pallas_call(
        functools.partial(_gla_kernel, tm=tm),
        grid=(b, s // tm),
        in_specs=[
            x_spec,
            _resident((1, d)),
            _resident(w_in.shape),
            _resident(w_g1.shape),
            _resident(w_g2.shape),
            _resident((1, GLA_KD)),
            _resident((1, GLA_DV)),
            _resident(w_out.shape),
        ],
        out_specs=x_spec,
        out_shape=jax.ShapeDtypeStruct((b, s, d), F32),
        scratch_shapes=[
            pltpu.VMEM((GLA_HEADS, GLA_DK, GLA_DV), F32),
            pltpu.VMEM((tm, GLA_VD), F32),
        ],
        compiler_params=pltpu.CompilerParams(
            dimension_semantics=("arbitrary", "arbitrary"), vmem_limit_bytes=VMEM_LIMIT),
        name="gla",
    )(x, gmix.reshape(1, d), w_in, w_g1, w_g2, b_g.reshape(1, GLA_KD),
      g_on.reshape(1, GLA_DV), w_out)


def _ffn_weights(w_gu, w_down):
    depth, d, _ = w_gu.shape
    n_chunks = D_FF // FFN_CHUNK
    w = w_gu.astype(BF16).reshape(depth, d, 2, n_chunks, FFN_CHUNK)
    w = jnp.transpose(w, (0, 2, 3, 1, 4))
    wd = w_down.astype(BF16).reshape(depth, n_chunks, FFN_CHUNK, d)
    return w[:, 0], w[:, 1], wd


def kernel(x, mem, n_ffn1, ffn1_w_gu, ffn1_w_down, n_mix, gla_w_in, gla_w_g1, gla_w_g2,
           gla_b_g, gla_onorm, gla_w_out, pool_w, pool_b, pool_scale, n_xattn, n_mem,
           x_w_q, x_w_kv, x_w_o, n_ffn2, ffn2_w_gu, ffn2_w_down, n_final):
    b, s, d = x.shape
    f1g, f1u, f1d = _ffn_weights(ffn1_w_gu, ffn1_w_down)
    f2g, f2u, f2d = _ffn_weights(ffn2_w_gu, ffn2_w_down)
    gla_w_in, gla_w_g1, gla_w_g2, gla_w_out = (
        w.astype(BF16) for w in (gla_w_in, gla_w_g1, gla_w_g2, gla_w_out))
    pool_w = pool_w.astype(BF16)
    x_w_q, x_w_o = x_w_q.astype(BF16), x_w_o.astype(BF16)
    kv = _mem_kv(mem, n_mem, x_w_kv.astype(BF16))

    for i in range(DEPTH):
        x = _ffn(x.reshape(b * s, d), n_ffn1[i], f1g[i], f1u[i], f1d[i]).reshape(b, s, d)
        j = i // 2
        if i % 2 == 0:
            x = _gla(x, n_mix[i], gla_w_in[j], gla_w_g1[j], gla_w_g2[j], gla_b_g[j],
                     gla_onorm[j], gla_w_out[j])
            pool_args = None
        else:
            pool_args = (n_mix[i], pool_w[j], pool_b[j], pool_scale[j])
        x = _xattn(x, i, kv, n_xattn[i], x_w_q[i], x_w_o[i], pool_args)
        g_final = n_final if i == DEPTH - 1 else None
        x = _ffn(x.reshape(b * s, d), n_ffn2[i], f2g[i], f2u[i], f2d[i], g_final).reshape(b, s, d)
    return x
```

```python
import functools

import jax
import jax.numpy as jnp
from jax import lax
from jax.experimental import pallas as pl
from jax.experimental.pallas import tpu as pltpu

F32 = jnp.float32
BF16 = jnp.bfloat16

D_MODEL = 1024
DEPTH = 4
EPS = 1e-6
D_FF = 2816
GLA_HEADS = 4
GLA_KD = 512
GLA_VD = 1024
GLA_DK = 128
GLA_DV = 256
GATE_TAU = 16.0
CHUNK = 64
POOL_WINDOWS = (2, 4, 8, 16)
POOL_GC = 256
POOL_HALO = 16
MEM_LEN = 256
X_HEADS = 4
X_DH = 256

MXU_TILE = 256
FFN_CHUNK = MXU_TILE
FFN_TM = 1024
XATTN_TM = 1024
GLA_TM = 512
VMEM_LIMIT = 56 * 1024 * 1024


def _rms(x, g):
    ms = jnp.mean(x * x, axis=-1, keepdims=True)
    return x * lax.rsqrt(ms + EPS) * g


def _layer_block(arr, layer):
    nd = arr.ndim
    return pl.BlockSpec((1,) + arr.shape[1:], lambda *_: (layer,) + (0,) * (nd - 1),
                        pipeline_mode=pl.Buffered(1))


def _dot(a, b):
    return jnp.dot(a, b, preferred_element_type=F32)


def _dot_nt(a, b):
    return lax.dot_general(a, b, (((1,), (1,)), ((), ())), preferred_element_type=F32)


def _params(semantics):
    return pltpu.CompilerParams(dimension_semantics=semantics, vmem_limit_bytes=VMEM_LIMIT)


def _ffn_kernel(x_ref, g_ref, wgu_ref, wd_ref, *rest, final_norm):
    if final_norm:
        gf_ref, o_ref = rest
    else:
        (o_ref,) = rest
    x = x_ref[...]
    h = _rms(x, g_ref[0]).astype(BF16)
    acc = None
    for c in range(D_FF // FFN_CHUNK):
        cols = slice(c * FFN_CHUNK, (c + 1) * FFN_CHUNK)
        g = _dot(h, wgu_ref[0, :, cols])
        u = _dot(h, wgu_ref[0, :, D_FF + c * FFN_CHUNK:D_FF + (c + 1) * FFN_CHUNK])
        a = (g * jax.nn.sigmoid(g) * u).astype(BF16)
        d = _dot(a, wd_ref[0, cols, :])
        acc = d if acc is None else acc + d
    y = x + 0.5 * acc
    if final_norm:
        y = _rms(y, gf_ref[...])
    o_ref[...] = y


def _ffn(x2d, layer, g, w_gu, w_down, g_final=None):
    t, d = x2d.shape
    final_norm = g_final is not None
    in_specs = [
        pl.BlockSpec((FFN_TM, d), lambda i: (i, 0)),
        _layer_block(g, layer),
        _layer_block(w_gu, layer),
        _layer_block(w_down, layer),
    ]
    args = [x2d, g, w_gu, w_down]
    if final_norm:
        in_specs.append(pl.BlockSpec((1, d), lambda i: (0, 0), pipeline_mode=pl.Buffered(1)))
        args.append(g_final.reshape(1, d))
    return pl.pallas_call(
        functools.partial(_ffn_kernel, final_norm=final_norm),
        grid=(t // FFN_TM,),
        in_specs=in_specs,
        out_specs=pl.BlockSpec((FFN_TM, d), lambda i: (i, 0)),
        out_shape=jax.ShapeDtypeStruct((t, d), F32),
        compiler_params=_params(("parallel",)),
        name="ffn",
    )(*args)


def _kv_kernel(mem_ref, g_ref, w_ref, o_ref):
    mn = _rms(mem_ref[0], g_ref[0]).astype(BF16)
    o_ref[0, 0] = _dot(mn, w_ref[0]).astype(BF16)


def _mem_kv(mem, n_mem, w_kv):
    b, m, d = mem.shape
    depth = w_kv.shape[0]
    return pl.pallas_call(
        _kv_kernel,
        grid=(depth, b),
        in_specs=[
            pl.BlockSpec((1, m, d), lambda l, i: (i, 0, 0)),
            pl.BlockSpec((1, 1, d), lambda l, i: (l, 0, 0)),
            pl.BlockSpec((1, d, 2 * d), lambda l, i: (l, 0, 0)),
        ],
        out_specs=pl.BlockSpec((1, 1, m, 2 * d), lambda l, i: (l, i, 0, 0)),
        out_shape=jax.ShapeDtypeStruct((depth, b, m, 2 * d), BF16),
        compiler_params=_params(("arbitrary", "arbitrary")),
        name="mem_kv",
    )(mem, n_mem, w_kv)


def _pool_mix(x, xp_ref, gmix_ref, pw_ref, pb_ref, ps_ref, tm):
    i = pl.program_id(1)
    gmix = gmix_ref[0]
    hp = _rms(x, gmix)
    hprev = jnp.where(i > 0, _rms(xp_ref[0], gmix), 0.0)
    ext = jnp.concatenate([hprev, hp], axis=0)
    pos = i * tm + lax.broadcasted_iota(jnp.int32, (tm, 1), 0)
    ys = []
    for gi, w in enumerate(POOL_WINDOWS):
        lanes = slice(gi * POOL_GC, (gi + 1) * POOL_GC)
        win = ext[:, lanes]
        k = 1
        while k < w:
            win = win + pltpu.roll(win, k, axis=0)
            k *= 2
        inv_cnt = 1.0 / jnp.minimum(pos + 1, w).astype(F32)
        p = win[POOL_HALO:] * inv_cnt - hp[:, lanes]
        ys.append(_dot(p.astype(BF16), pw_ref[0, gi]) + pb_ref[0, gi:gi + 1, :])
    y = jnp.concatenate(ys, axis=1) * ps_ref[0]
    return x + y


def _xattn_kernel(*refs, pool, tm):
    if pool:
        (x_ref, xp_ref, gmix_ref, pw_ref, pb_ref, ps_ref,
         gx_ref, wq_ref, kv_ref, wo_ref, o_ref) = refs
    else:
        x_ref, gx_ref, wq_ref, kv_ref, wo_ref, o_ref = refs
    x = x_ref[0]
    if pool:
        x = _pool_mix(x, xp_ref, gmix_ref, pw_ref, pb_ref, ps_ref, tm)
    hn = _rms(x, gx_ref[0]).astype(BF16)
    q = (_dot(hn, wq_ref[0]) * (X_DH ** -0.5)).astype(BF16)
    outs = []
    for h in range(X_HEADS):
        lanes = slice(h * X_DH, (h + 1) * X_DH)
        kh = kv_ref[0, 0, :, h * X_DH:(h + 1) * X_DH]
        vh = kv_ref[0, 0, :, D_MODEL + h * X_DH:D_MODEL + (h + 1) * X_DH]
        s = _dot_nt(q[:, lanes], kh)
        e = jnp.exp(s - jnp.max(s, axis=-1, keepdims=True))
        p = e * (1.0 / jnp.sum(e, axis=-1, keepdims=True))
        outs.append(_dot(p.astype(BF16), vh).astype(BF16))
    o = jnp.concatenate(outs, axis=1)
    o_ref[0] = x + _dot(o, wo_ref[0])


def _xattn(x, layer, kv, gx, wq, wo, pool_args=None):
    b, s, d = x.shape
    tm = XATTN_TM
    pool = pool_args is not None
    x_spec = pl.BlockSpec((1, tm, d), lambda bi, i: (bi, i, 0))
    in_specs = [x_spec]
    args = [x]
    if pool:
        pool_layer, gmix, pw, pb, ps = pool_args
        halo_blocks = tm // POOL_HALO
        in_specs += [
            pl.BlockSpec((1, POOL_HALO, d),
                         lambda bi, i: (bi, jnp.maximum(i * halo_blocks - 1, 0), 0)),
            _layer_block(gmix, layer),
            _layer_block(pw, pool_layer),
            _layer_block(pb, pool_layer),
            _layer_block(ps, pool_layer),
        ]
        args += [x, gmix, pw, pb, ps]
    in_specs += [
        _layer_block(gx, layer),
        _layer_block(wq, layer),
        pl.BlockSpec((1, 1, MEM_LEN, 2 * d), lambda bi, i: (layer, bi, 0, 0)),
        _layer_block(wo, layer),
    ]
    args += [gx, wq, kv, wo]
    return pl.pallas_call(
        functools.partial(_xattn_kernel, pool=pool, tm=tm),
        grid=(b, s // tm),
        in_specs=in_specs,
        out_specs=x_spec,
        out_shape=jax.ShapeDtypeStruct((b, s, d), F32),
        compiler_params=_params(("arbitrary", "arbitrary")),
        name="pool_xattn" if pool else "xattn",
    )(*args)


def _gla_kernel(x_ref, gmix_ref, win_ref, wg1_ref, wg2_ref, bg_ref, gon_ref, wout_ref,
                o_ref, state_ref, *, tm):
    @pl.when(pl.program_id(1) == 0)
    def _():
        state_ref[...] = jnp.zeros_like(state_ref)

    nc = tm // CHUNK
    x = x_ref[0]
    h = _rms(x, gmix_ref[0]).astype(BF16)
    z = _dot(_dot(h, wg1_ref[0]).astype(BF16), wg2_ref[0]) + bg_ref[0]
    v = _dot(h, win_ref[0, :, 2 * GLA_KD:2 * GLA_KD + GLA_VD]).astype(BF16)
    r = _dot(h, win_ref[0, :, 2 * GLA_KD + GLA_VD:])
    glog = jax.nn.log_sigmoid(z) * (1.0 / GATE_TAU)
    g_hi = glog.astype(BF16)
    g_lo = (glog - g_hi.astype(F32)).astype(BF16)
    row = lax.broadcasted_iota(jnp.int32, (CHUNK, CHUNK), 0)
    col = lax.broadcasted_iota(jnp.int32, (CHUNK, CHUNK), 1)
    causal = row >= col
    tril = causal.astype(BF16)
    tril2 = jnp.concatenate([tril, tril], axis=1)
    chunk_rows = [slice(c * CHUNK, (c + 1) * CHUNK) for c in range(nc)]
    heads = [(slice(hh * GLA_DK, (hh + 1) * GLA_DK), slice(hh * GLA_DV, (hh + 1) * GLA_DV))
             for hh in range(GLA_HEADS)]

    b = jnp.concatenate(
        [_dot(tril2, jnp.concatenate([g_hi[rows], g_lo[rows]], axis=0)) for rows in chunk_rows],
        axis=0)
    b = b.reshape(nc, CHUNK, GLA_KD)
    b_ref = b[:, CHUNK // 2 - 1:CHUNK // 2, :]
    b_last = b[:, CHUNK - 1:CHUNK, :]
    qk = _dot(h, win_ref[0, :, 0:2 * GLA_KD])
    q = (qk[:, 0:GLA_KD] * (GLA_DK ** -0.5)).reshape(nc, CHUNK, GLA_KD)
    k = qk[:, GLA_KD:].reshape(nc, CHUNK, GLA_KD)
    qa = (q * jnp.exp(b - b_ref)).astype(BF16)
    kb = (k * jnp.exp(b_ref - b)).astype(BF16)
    qi = (q * jnp.exp(b)).astype(BF16)
    kst = k * jnp.exp(b_last - b)

    kst_t = [kst[c].T.astype(BF16) for c in range(nc)]
    decay = [jnp.exp(b[c, CHUNK - 8:CHUNK, :].T[:, 7:8]) for c in range(nc)]
    att = [[jnp.where(causal, _dot_nt(qa[c][:, kl], kb[c][:, kl]), 0.0).astype(BF16)
            for kl, _ in heads] for c in range(nc)]
    upd = [[_dot(kst_t[c][kl], v[chunk_rows[c], vl]) for kl, vl in heads] for c in range(nc)]

    o_rows = [[None] * GLA_HEADS for _ in range(nc)]
    for hh, (kl, vl) in enumerate(heads):
        st = state_ref[hh]
        for c in range(nc):
            lhs = jnp.concatenate([qi[c][:, kl], att[c][hh]], axis=1)
            rhs = jnp.concatenate([st.astype(BF16), v[chunk_rows[c], vl]], axis=0)
            o_rows[c][hh] = _dot(lhs, rhs)
            st = decay[c][kl] * st + upd[c][hh]
        state_ref[hh] = st
    o = jnp.concatenate([jnp.concatenate(o_rows[c], axis=1) for c in range(nc)], axis=0)

    gate = r * jax.nn.sigmoid(r)
    gon = gon_ref[0]
    outs = [(_rms(o[:, vl], gon) * gate[:, vl]).astype(BF16) for _, vl in heads]
    o_ref[0] = x + _dot(jnp.concatenate(outs, axis=1), wout_ref[0])


def _gla(x, layer, gla_layer, gmix, w_in, w_g1, w_g2, b_g, g_on, w_out):
    b, s, d = x.shape
    tm = GLA_TM
    x_spec = pl.BlockSpec((1, tm, d), lambda bi, i: (bi, i, 0))
    return pl.pallas_call(
        functools.partial(_gla_kernel, tm=tm),
        grid=(b, s // tm),
        in_specs=[
            x_spec,
            _layer_block(gmix, layer),
            _layer_block(w_in, gla_layer),
            _layer_block(w_g1, gla_layer),
            _layer_block(w_g2, gla_layer),
            _layer_block(b_g, gla_layer),
            _layer_block(g_on, gla_layer),
            _layer_block(w_out, gla_layer),
        ],
        out_specs=x_spec,
        out_shape=jax.ShapeDtypeStruct((b, s, d), F32),
        scratch_shapes=[pltpu.VMEM((GLA_HEADS, GLA_DK, GLA_DV), F32)],
        compiler_params=_params(("arbitrary", "arbitrary")),
        name="gla",
    )(x, gmix, w_in, w_g1, w_g2, b_g, g_on, w_out)


def _rows(p):
    return p.reshape(p.shape[0], 1, p.shape[1])


def kernel(x, mem, n_ffn1, ffn1_w_gu, ffn1_w_down, n_mix, gla_w_in, gla_w_g1, gla_w_g2,
           gla_b_g, gla_onorm, gla_w_out, pool_w, pool_b, pool_scale, n_xattn, n_mem,
           x_w_q, x_w_kv, x_w_o, n_ffn2, ffn2_w_gu, ffn2_w_down, n_final):
    b, s, d = x.shape
    (ffn1_w_gu, ffn1_w_down, ffn2_w_gu, ffn2_w_down, gla_w_in, gla_w_g1, gla_w_g2, gla_w_out,
     pool_w, x_w_q, x_w_kv, x_w_o) = (
        w.astype(BF16) for w in (ffn1_w_gu, ffn1_w_down, ffn2_w_gu, ffn2_w_down, gla_w_in,
                                 gla_w_g1, gla_w_g2, gla_w_out, pool_w, x_w_q, x_w_kv, x_w_o))
    n_ffn1, n_mix, n_xattn, n_mem, n_ffn2, gla_b_g, gla_onorm, pool_scale = (
        _rows(p) for p in (n_ffn1, n_mix, n_xattn, n_mem, n_ffn2, gla_b_g, gla_onorm, pool_scale))
    kv = _mem_kv(mem, n_mem, x_w_kv)

    for i in range(DEPTH):
        x = _ffn(x.reshape(b * s, d), i, n_ffn1, ffn1_w_gu, ffn1_w_down).reshape(b, s, d)
        j = i // 2
        if i % 2 == 0:
            x = _gla(x, i, j, n_mix, gla_w_in, gla_w_g1, gla_w_g2, gla_b_g, gla_onorm, gla_w_out)
            pool_args = None
        else:
            pool_args = (j, n_mix, pool_w, pool_b, pool_scale)
        x = _xattn(x, i, kv, n_xattn, x_w_q, x_w_o, pool_args)
        g_final = n_final if i == DEPTH - 1 else None
        x = _ffn(x.reshape(b * s, d), i, n_ffn2, ffn2_w_gu, ffn2_w_down, g_final).reshape(b, s, d)
    return x
```

```python
import functools

import jax
import jax.numpy as jnp
from jax import lax
from jax.experimental import pallas as pl
from jax.experimental.pallas import tpu as pltpu

F32 = jnp.float32
BF16 = jnp.bfloat16

D_MODEL = 1024
DEPTH = 4
EPS = 1e-6
D_FF = 2816
GLA_HEADS = 4
GLA_KD = 512
GLA_VD = 1024
GLA_DK = 128
GLA_DV = 256
GATE_TAU = 16.0
CHUNK = 64
POOL_WINDOWS = (2, 4, 8, 16)
POOL_GC = 256
POOL_HALO = 16
MEM_LEN = 256
X_HEADS = 4
X_DH = 256

MXU_TILE = 256
FFN_CHUNK = MXU_TILE
FFN_TM = 1024
XATTN_TM = 512
PREP_ROWS = 128
GLA_TM = 1024
VMEM_LIMIT = 56 * 1024 * 1024


def _rms(x, g):
    ms = jnp.mean(x * x, axis=-1, keepdims=True)
    return x * lax.rsqrt(ms + EPS) * g


def _layer_block(arr, layer):
    nd = arr.ndim
    return pl.BlockSpec((1,) + arr.shape[1:], lambda *_: (layer,) + (0,) * (nd - 1),
                        pipeline_mode=pl.Buffered(1))


def _dot(a, b):
    return jnp.dot(a, b, preferred_element_type=F32)


def _dot_nt(a, b):
    return lax.dot_general(a, b, (((1,), (1,)), ((), ())), preferred_element_type=F32)


def _params(semantics):
    return pltpu.CompilerParams(dimension_semantics=semantics, vmem_limit_bytes=VMEM_LIMIT)


def _ffn_kernel(x_ref, g_ref, wgu_ref, wd_ref, *rest, final_norm):
    if final_norm:
        gf_ref, o_ref = rest
    else:
        (o_ref,) = rest
    x = x_ref[...]
    h = _rms(x, g_ref[0]).astype(BF16)
    acc = None
    for c in range(D_FF // FFN_CHUNK):
        cols = slice(c * FFN_CHUNK, (c + 1) * FFN_CHUNK)
        g = _dot(h, wgu_ref[0, :, cols])
        u = _dot(h, wgu_ref[0, :, D_FF + c * FFN_CHUNK:D_FF + (c + 1) * FFN_CHUNK])
        a = (g * jax.nn.sigmoid(g) * u).astype(BF16)
        d = _dot(a, wd_ref[0, cols, :])
        acc = d if acc is None else acc + d
    y = x + 0.5 * acc
    if final_norm:
        y = _rms(y, gf_ref[...])
    o_ref[...] = y


def _ffn(x2d, layer, g, w_gu, w_down, g_final=None):
    t, d = x2d.shape
    final_norm = g_final is not None
    in_specs = [
        pl.BlockSpec((FFN_TM, d), lambda i: (i, 0)),
        _layer_block(g, layer),
        _layer_block(w_gu, layer),
        _layer_block(w_down, layer),
    ]
    args = [x2d, g, w_gu, w_down]
    if final_norm:
        in_specs.append(pl.BlockSpec((1, d), lambda i: (0, 0), pipeline_mode=pl.Buffered(1)))
        args.append(g_final.reshape(1, d))
    return pl.pallas_call(
        functools.partial(_ffn_kernel, final_norm=final_norm),
        grid=(t // FFN_TM,),
        in_specs=in_specs,
        out_specs=pl.BlockSpec((FFN_TM, d), lambda i: (i, 0)),
        out_shape=jax.ShapeDtypeStruct((t, d), F32),
        compiler_params=_params(("parallel",)),
        name="ffn",
    )(*args)


def _kv_kernel(mem_ref, g_ref, w_ref, o_ref):
    mn = _rms(mem_ref[...], g_ref[0]).astype(BF16)
    o_ref[0] = _dot(mn, w_ref[0]).astype(BF16)


def _mem_kv(mem, n_mem, w_kv):
    b, m, d = mem.shape
    depth = w_kv.shape[0]
    kv = pl.pallas_call(
        _kv_kernel,
        grid=(depth,),
        in_specs=[
            pl.BlockSpec((b * m, d), lambda l: (0, 0), pipeline_mode=pl.Buffered(1)),
            pl.BlockSpec((1, 1, d), lambda l: (l, 0, 0)),
            pl.BlockSpec((1, d, 2 * d), lambda l: (l, 0, 0)),
        ],
        out_specs=pl.BlockSpec((1, b * m, 2 * d), lambda l: (l, 0, 0)),
        out_shape=jax.ShapeDtypeStruct((depth, b * m, 2 * d), BF16),
        compiler_params=_params(("arbitrary",)),
        name="mem_kv",
    )(mem.reshape(b * m, d), n_mem, w_kv)
    return kv.reshape(depth, b, m, 2 * d)


def _pool_mix(x, hprev, seq_start, gmix, pw_ref, pb_ref, ps_ref):
    rows = x.shape[0]
    hp = _rms(x, gmix)
    ext = jnp.concatenate([hprev, hp], axis=0)
    row = lax.broadcasted_iota(jnp.int32, (rows, 1), 0)
    ys = []
    for gi, w in enumerate(POOL_WINDOWS):
        lanes = slice(gi * POOL_GC, (gi + 1) * POOL_GC)
        win = ext[:, lanes]
        k = 1
        while k < w:
            win = win + pltpu.roll(win, k, axis=0)
            k *= 2
        if seq_start is False:
            inv_cnt = 1.0 / w
        else:
            inv_cnt = 1.0 / jnp.where(seq_start, jnp.minimum(row + 1, w), w).astype(F32)
        p = win[POOL_HALO:] * inv_cnt - hp[:, lanes]
        ys.append(_dot(p.astype(BF16), pw_ref[0, gi]) + pb_ref[0, gi:gi + 1, :])
    y = jnp.concatenate(ys, axis=1) * ps_ref[0]
    return x + y, hp[rows - POOL_HALO:]


def _interleave(main, side):
    for _ in main:
        next(side, None)
    for _ in side:
        pass


def _xattn_kernel(*refs, pool, tm, tiles_per_seq):
    if pool:
        (x_ref, xn_ref, gmix_ref, pw_ref, pb_ref, ps_ref, gx_ref, wq_ref, kv_ref, wo_ref,
         o_ref, hna_ref, hnb_ref, x1a_ref, x1b_ref) = refs
    else:
        x_ref, xn_ref, gx_ref, wq_ref, kv_ref, wo_ref, o_ref, hna_ref, hnb_ref = refs
        x1a_ref = x1b_ref = None
    t = pl.program_id(0)

    def prepare(src_ref, src_row0, halo, seq_start, hn_ref, x1_ref):
        if pool:
            gmix = gmix_ref[0]
            hprev = jnp.where(seq_start, 0.0, _rms(halo, gmix))
        for c in range(tm // PREP_ROWS):
            r0 = c * PREP_ROWS
            xt = src_ref[src_row0 + r0:src_row0 + r0 + PREP_ROWS, :]
            if pool:
                xt, hprev = _pool_mix(xt, hprev, seq_start if c == 0 else False,
                                      gmix, pw_ref, pb_ref, ps_ref)
                x1_ref[r0:r0 + PREP_ROWS, :] = xt
            hn_ref[r0:r0 + PREP_ROWS, :] = _rms(xt, gx_ref[0]).astype(BF16)
            yield

    def attend(row0, hn_ref, x1_ref):
        pair = 2 * X_DH
        outs = []
        for hp in range(X_HEADS // 2):
            cols = slice(hp * pair, (hp + 1) * pair)
            q = (_dot(hn_ref[...], wq_ref[0, :, cols]) * (X_DH ** -0.5)).astype(BF16)
            yield
            for h in (2 * hp, 2 * hp + 1):
                kh = kv_ref[0, 0, :, h * X_DH:(h + 1) * X_DH]
                vh = kv_ref[0, 0, :, D_MODEL + h * X_DH:D_MODEL + (h + 1) * X_DH]
                s = _dot_nt(q[:, (h % 2) * X_DH:(h % 2 + 1) * X_DH], kh)
                e = jnp.exp(s - jnp.max(s, axis=-1, keepdims=True))
                p = e * (1.0 / jnp.sum(e, axis=-1, keepdims=True))
                outs.append(_dot(p.astype(BF16), vh).astype(BF16))
            yield
        o = jnp.concatenate(outs, axis=1)
        for hp in range(X_HEADS // 2):
            cols = slice(hp * pair, (hp + 1) * pair)
            x1 = x1_ref[:, cols] if pool else x_ref[row0:row0 + tm, cols]
            o_ref[row0:row0 + tm, cols] = x1 + _dot(o, wo_ref[0, :, cols])
            yield

    @pl.when(t == 0)
    def _():
        for _ in prepare(x_ref, 0, x_ref[0:POOL_HALO, :], True, hna_ref, x1a_ref):
            pass

    _interleave(attend(0, hna_ref, x1a_ref),
                prepare(x_ref, tm, x_ref[tm - POOL_HALO:tm, :], False, hnb_ref, x1b_ref))
    next_opens_seq = lax.rem(2 * t + 2, tiles_per_seq) == 0
    _interleave(attend(tm, hnb_ref, x1b_ref),
                prepare(xn_ref, 0, x_ref[2 * tm - POOL_HALO:2 * tm, :], next_opens_seq,
                        hna_ref, x1a_ref))


def _xattn(x, layer, kv, gx, wq, wo, pool_args=None):
    b, s, d = x.shape
    tm = XATTN_TM
    n_tiles = b * s // tm
    tiles_per_seq = s // tm
    assert tiles_per_seq % 2 == 0
    pool = pool_args is not None
    step_spec = pl.BlockSpec((2 * tm, d), lambda t: (t, 0))
    in_specs = [step_spec,
                pl.BlockSpec((tm, d), lambda t: (jnp.minimum(2 * t + 2, n_tiles - 1), 0))]
    x2d = x.reshape(b * s, d)
    args = [x2d, x2d]
    scratch = [pltpu.VMEM((tm, d), BF16), pltpu.VMEM((tm, d), BF16)]
    if pool:
        pool_layer, gmix, pw, pb, ps = pool_args
        in_specs += [_layer_block(gmix, layer), _layer_block(pw, pool_layer),
                     _layer_block(pb, pool_layer), _layer_block(ps, pool_layer)]
        args += [gmix, pw, pb, ps]
        scratch += [pltpu.VMEM((tm, d), F32), pltpu.VMEM((tm, d), F32)]
    in_specs += [
        _layer_block(gx, layer),
        _layer_block(wq, layer),
        pl.BlockSpec((1, 1, MEM_LEN, 2 * d), lambda t: (layer, (2 * t) // tiles_per_seq, 0, 0)),
        _layer_block(wo, layer),
    ]
    args += [gx, wq, kv, wo]
    out = pl.pallas_call(
        functools.partial(_xattn_kernel, pool=pool, tm=tm, tiles_per_seq=tiles_per_seq),
        grid=(n_tiles // 2,),
        in_specs=in_specs,
        out_specs=step_spec,
        out_shape=jax.ShapeDtypeStruct((b * s, d), F32),
        scratch_shapes=scratch,
        compiler_params=_params(("arbitrary",)),
        name="pool_xattn" if pool else "xattn",
    )(*args)
    return out.reshape(b, s, d)


def _gla_kernel(x_ref, gmix_ref, win_ref, wg1_ref, wg2_ref, bg_ref, gon_ref, wout_ref,
                o_ref, state_ref, *, tm):
    @pl.when(pl.program_id(1) == 0)
    def _():
        state_ref[...] = jnp.zeros_like(state_ref)

    nc = tm // CHUNK
    x = x_ref[0]
    h = _rms(x, gmix_ref[0]).astype(BF16)
    z = _dot(_dot(h, wg1_ref[0]).astype(BF16), wg2_ref[0]) + bg_ref[0]
    v = _dot(h, win_ref[0, :, 2 * GLA_KD:2 * GLA_KD + GLA_VD]).astype(BF16)
    r = _dot(h, win_ref[0, :, 2 * GLA_KD + GLA_VD:])
    glog = jax.nn.log_sigmoid(z) * (1.0 / GATE_TAU)
    g_hi = glog.astype(BF16)
    g_lo = (glog - g_hi.astype(F32)).astype(BF16)
    row = lax.broadcasted_iota(jnp.int32, (CHUNK, CHUNK), 0)
    col = lax.broadcasted_iota(jnp.int32, (CHUNK, CHUNK), 1)
    causal = row >= col
    tril = causal.astype(BF16)
    tril2 = jnp.concatenate([tril, tril], axis=1)
    chunk_rows = [slice(c * CHUNK, (c + 1) * CHUNK) for c in range(nc)]
    heads = [(slice(hh * GLA_DK, (hh + 1) * GLA_DK), slice(hh * GLA_DV, (hh + 1) * GLA_DV))
             for hh in range(GLA_HEADS)]

    b = jnp.concatenate(
        [_dot(tril2, jnp.concatenate([g_hi[rows], g_lo[rows]], axis=0)) for rows in chunk_rows],
        axis=0)
    b = b.reshape(nc, CHUNK, GLA_KD)
    b_ref = b[:, CHUNK // 2 - 1:CHUNK // 2, :]
    b_last = b[:, CHUNK - 1:CHUNK, :]
    qk = _dot(h, win_ref[0, :, 0:2 * GLA_KD])
    q = (qk[:, 0:GLA_KD] * (GLA_DK ** -0.5)).reshape(nc, CHUNK, GLA_KD)
    k = qk[:, GLA_KD:].reshape(nc, CHUNK, GLA_KD)
    qa = (q * jnp.exp(b - b_ref)).astype(BF16)
    kb = (k * jnp.exp(b_ref - b)).astype(BF16)
    qi = (q * jnp.exp(b)).astype(BF16)
    kst = k * jnp.exp(b_last - b)

    kst_t = [kst[c].T.astype(BF16) for c in range(nc)]
    decay = [jnp.exp(b[c, CHUNK - 8:CHUNK, :].T[:, 7:8]) for c in range(nc)]
    att = [[jnp.where(causal, _dot_nt(qa[c][:, kl], kb[c][:, kl]), 0.0).astype(BF16)
            for kl, _ in heads] for c in range(nc)]
    upd = [[_dot(kst_t[c][kl], v[chunk_rows[c], vl]) for kl, vl in heads] for c in range(nc)]

    o_rows = [[None] * GLA_HEADS for _ in range(nc)]
    for hh, (kl, vl) in enumerate(heads):
        st = state_ref[hh]
        for c in range(nc):
            lhs = jnp.concatenate([qi[c][:, kl], att[c][hh]], axis=1)
            rhs = jnp.concatenate([st.astype(BF16), v[chunk_rows[c], vl]], axis=0)
            o_rows[c][hh] = _dot(lhs, rhs)
            st = decay[c][kl] * st + upd[c][hh]
        state_ref[hh] = st
    o = jnp.concatenate([jnp.concatenate(o_rows[c], axis=1) for c in range(nc)], axis=0)

    gate = r * jax.nn.sigmoid(r)
    gon = gon_ref[0]
    outs = [(_rms(o[:, vl], gon) * gate[:, vl]).astype(BF16) for _, vl in heads]
    o_ref[0] = x + _dot(jnp.concatenate(outs, axis=1), wout_ref[0])


def _gla(x, layer, gla_layer, gmix, w_in, w_g1, w_g2, b_g, g_on, w_out):
    b, s, d = x.shape
    tm = GLA_TM
    x_spec = pl.BlockSpec((1, tm, d), lambda bi, i: (bi, i, 0))
    return pl.pallas_call(
        functools.partial(_gla_kernel, tm=tm),
        grid=(b, s // tm),
        in_specs=[
            x_spec,
            _layer_block(gmix, layer),
            _layer_block(w_in, gla_layer),
            _layer_block(w_g1, gla_layer),
            _layer_block(w_g2, gla_layer),
            _layer_block(b_g, gla_layer),
            _layer_block(g_on, gla_layer),
            _layer_block(w_out, gla_layer),
        ],
        out_specs=x_spec,
        out_shape=jax.ShapeDtypeStruct((b, s, d), F32),
        scratch_shapes=[pltpu.VMEM((GLA_HEADS, GLA_DK, GLA_DV), F32)],
        compiler_params=_params(("arbitrary", "arbitrary")),
        name="gla",
    )(x, gmix, w_in, w_g1, w_g2, b_g, g_on, w_out)


def _rows(p):
    return p.reshape(p.shape[0], 1, p.shape[1])


def kernel(x, mem, n_ffn1, ffn1_w_gu, ffn1_w_down, n_mix, gla_w_in, gla_w_g1, gla_w_g2,
           gla_b_g, gla_onorm, gla_w_out, pool_w, pool_b, pool_scale, n_xattn, n_mem,
           x_w_q, x_w_kv, x_w_o, n_ffn2, ffn2_w_gu, ffn2_w_down, n_final):
    b, s, d = x.shape
    (ffn1_w_gu, ffn1_w_down, ffn2_w_gu, ffn2_w_down, gla_w_in, gla_w_g1, gla_w_g2, gla_w_out,
     pool_w, x_w_q, x_w_kv, x_w_o) = (
        w.astype(BF16) for w in (ffn1_w_gu, ffn1_w_down, ffn2_w_gu, ffn2_w_down, gla_w_in,
                                 gla_w_g1, gla_w_g2, gla_w_out, pool_w, x_w_q, x_w_kv, x_w_o))
    n_ffn1, n_mix, n_xattn, n_mem, n_ffn2, gla_b_g, gla_onorm, pool_scale = (
        _rows(p) for p in (n_ffn1, n_mix, n_xattn, n_mem, n_ffn2, gla_b_g, gla_onorm, pool_scale))
    kv = _mem_kv(mem, n_mem, x_w_kv)

    for i in range(DEPTH):
        x = _ffn(x.reshape(b * s, d), i, n_ffn1, ffn1_w_gu, ffn1_w_down).reshape(b, s, d)
        j = i // 2
        if i % 2 == 0:
            x = _gla(x, i, j, n_mix, gla_w_in, gla_w_g1, gla_w_g2, gla_b_g, gla_onorm, gla_w_out)
            pool_args = None
        else:
            pool_args = (j, n_mix, pool_w, pool_b, pool_scale)
        x = _xattn(x, i, kv, n_xattn, x_w_q, x_w_o, pool_args)
        g_final = n_final if i == DEPTH - 1 else None
        x = _ffn(x.reshape(b * s, d), i, n_ffn2, ffn2_w_gu, ffn2_w_down, g_final).reshape(b, s, d)
    return x
```

```python
import functools

import jax
import jax.numpy as jnp
from jax import lax
from jax.experimental import pallas as pl
from jax.experimental.pallas import tpu as pltpu

F32 = jnp.float32
BF16 = jnp.bfloat16

D_MODEL = 1024
DEPTH = 4
EPS = 1e-6
D_FF = 2816
GLA_HEADS = 4
GLA_KD = 512
GLA_VD = 1024
GLA_DK = 128
GLA_DV = 256
GATE_TAU = 16.0
CHUNK = 64
POOL_WINDOWS = (2, 4, 8, 16)
POOL_GC = 256
POOL_HALO = 16
MEM_LEN = 256
X_HEADS = 4
X_DH = 256

MXU_TILE = 256
BF16_ROWS = 16
FFN_CHUNK = MXU_TILE
FFN_TM = 1024
XATTN_TM = 1024
POOL_TM = 512
PREP_ROWS = 128
GLA_TM = 1024
VMEM_LIMIT = 56 * 1024 * 1024


def _rms(x, g):
    ms = jnp.mean(x * x, axis=-1, keepdims=True)
    return x * lax.rsqrt(ms + EPS) * g


def _layer_block(arr, layer):
    nd = arr.ndim
    return pl.BlockSpec((1,) + arr.shape[1:], lambda *_: (layer,) + (0,) * (nd - 1),
                        pipeline_mode=pl.Buffered(1))


def _dot(a, b):
    return jnp.dot(a, b, preferred_element_type=F32)


def _dot_nt(a, b):
    return lax.dot_general(a, b, (((1,), (1,)), ((), ())), preferred_element_type=F32)


def _params(semantics):
    return pltpu.CompilerParams(dimension_semantics=semantics, vmem_limit_bytes=VMEM_LIMIT)


def _cast_plan(w, layer, n_steps):
    layers, rows, cols = w.shape
    rpb = next(k for k in range(BF16_ROWS, rows + 1, BF16_ROWS)
               if rows % k == 0 and rows // k <= n_steps)
    nblk = rows // rpb
    src = w.reshape(layers, nblk, rpb, cols)
    in_spec = pl.BlockSpec((1, 1, rpb, cols), lambda i: (layer, jnp.minimum(i, nblk - 1), 0, 0))
    out_spec = pl.BlockSpec((1, rpb, cols), lambda i: (jnp.minimum(i, nblk - 1), 0, 0))
    return src, in_spec, out_spec, jax.ShapeDtypeStruct((nblk, rpb, cols), BF16)


def _ffn_kernel(*refs, final_norm, n_casts):
    x_ref, g_ref, wgu_ref, wd_ref = refs[:4]
    rest = refs[4:]
    if final_norm:
        gf_ref, rest = rest[0], rest[1:]
    cast_src, o_ref, cast_dst = rest[:n_casts], rest[n_casts], rest[n_casts + 1:]
    for src_ref, dst_ref in zip(cast_src, cast_dst):
        dst_ref[0] = src_ref[0, 0].astype(BF16)
    x = x_ref[...]
    h = _rms(x, g_ref[0]).astype(BF16)
    acc = None
    for c in range(D_FF // FFN_CHUNK):
        cols = slice(c * FFN_CHUNK, (c + 1) * FFN_CHUNK)
        g = _dot(h, wgu_ref[0, :, cols])
        u = _dot(h, wgu_ref[0, :, D_FF + c * FFN_CHUNK:D_FF + (c + 1) * FFN_CHUNK])
        a = (g * jax.nn.sigmoid(g) * u).astype(BF16)
        d = _dot(a, wd_ref[0, cols, :])
        acc = d if acc is None else acc + d
    y = x + 0.5 * acc
    if final_norm:
        y = _rms(y, gf_ref[...])
    o_ref[...] = y


def _ffn(x, layer, g, w_gu, w_down, g_final=None, casts=()):
    b, s, d = x.shape
    t = b * s
    n_steps = t // FFN_TM
    final_norm = g_final is not None
    row_spec = pl.BlockSpec((FFN_TM, d), lambda i: (i, 0))
    in_specs = [row_spec, _layer_block(g, layer), _layer_block(w_gu, 0), _layer_block(w_down, 0)]
    args = [x.reshape(t, d), g, w_gu, w_down]
    if final_norm:
        in_specs.append(pl.BlockSpec((1, d), lambda i: (0, 0), pipeline_mode=pl.Buffered(1)))
        args.append(g_final.reshape(1, d))
    out_specs = [row_spec]
    out_shapes = [jax.ShapeDtypeStruct((t, d), F32)]
    for w, w_layer in casts:
        src, in_spec, out_spec, out_shape = _cast_plan(w, w_layer, n_steps)
        args.append(src)
        in_specs.append(in_spec)
        out_specs.append(out_spec)
        out_shapes.append(out_shape)
    outs = pl.pallas_call(
        functools.partial(_ffn_kernel, final_norm=final_norm, n_casts=len(casts)),
        grid=(n_steps,),
        in_specs=in_specs,
        out_specs=out_specs,
        out_shape=out_shapes,
        compiler_params=_params(("arbitrary",)),
        name="ffn",
    )(*args)
    cast_outs = [o.reshape(1, w.shape[1], w.shape[2]) for o, (w, _) in zip(outs[1:], casts)]
    return outs[0].reshape(b, s, d), cast_outs


def _kv_kernel(mem_ref, g_ref, w_ref, o_ref):
    mn = _rms(mem_ref[...], g_ref[0]).astype(BF16)
    o_ref[0] = _dot(mn, w_ref[0]).astype(BF16)


def _mem_kv(mem, n_mem, w_kv):
    b, m, d = mem.shape
    depth = w_kv.shape[0]
    kv = pl.pallas_call(
        _kv_kernel,
        grid=(depth,),
        in_specs=[
            pl.BlockSpec((b * m, d), lambda l: (0, 0), pipeline_mode=pl.Buffered(1)),
            pl.BlockSpec((1, 1, d), lambda l: (l, 0, 0)),
            pl.BlockSpec((1, d, 2 * d), lambda l: (l, 0, 0)),
        ],
        out_specs=pl.BlockSpec((1, b * m, 2 * d), lambda l: (l, 0, 0)),
        out_shape=jax.ShapeDtypeStruct((depth, b * m, 2 * d), BF16),
        compiler_params=_params(("arbitrary",)),
        name="mem_kv",
    )(mem.reshape(b * m, d), n_mem, w_kv)
    return kv.reshape(depth, b, m, 2 * d)


def _attend_heads(q, kv_ref, heads):
    outs = []
    for n, h in enumerate(heads):
        kh = kv_ref[0, 0, :, h * X_DH:(h + 1) * X_DH]
        vh = kv_ref[0, 0, :, D_MODEL + h * X_DH:D_MODEL + (h + 1) * X_DH]
        s = _dot_nt(q[:, n * X_DH:(n + 1) * X_DH], kh)
        e = jnp.exp(s - jnp.max(s, axis=-1, keepdims=True))
        p = e * (1.0 / jnp.sum(e, axis=-1, keepdims=True))
        outs.append(_dot(p.astype(BF16), vh).astype(BF16))
    return outs


def _xattn_kernel(x_ref, gx_ref, wq_ref, kv_ref, wo_ref, o_ref):
    x = x_ref[0]
    hn = _rms(x, gx_ref[0]).astype(BF16)
    q = (_dot(hn, wq_ref[0]) * (X_DH ** -0.5)).astype(BF16)
    o = jnp.concatenate(_attend_heads(q, kv_ref, range(X_HEADS)), axis=1)
    o_ref[0] = x + _dot(o, wo_ref[0])


def _xattn(x, layer, kv, gx, wq, wo):
    b, s, d = x.shape
    tm = XATTN_TM
    x_spec = pl.BlockSpec((1, tm, d), lambda bi, i: (bi, i, 0))
    return pl.pallas_call(
        _xattn_kernel,
        grid=(b, s // tm),
        in_specs=[
            x_spec,
            _layer_block(gx, layer),
            _layer_block(wq, 0),
            pl.BlockSpec((1, 1, MEM_LEN, 2 * d), lambda bi, i: (layer, bi, 0, 0)),
            _layer_block(wo, 0),
        ],
        out_specs=x_spec,
        out_shape=jax.ShapeDtypeStruct((b, s, d), F32),
        compiler_params=_params(("arbitrary", "arbitrary")),
        name="xattn",
    )(x, gx, wq, kv, wo)


def _pool_mix(x, hprev, seq_start, gmix, pw_ref, pb_ref, ps_ref):
    rows = x.shape[0]
    hp = _rms(x, gmix)
    ext = jnp.concatenate([hprev, hp], axis=0)
    row = lax.broadcasted_iota(jnp.int32, (rows, 1), 0)
    ys = []
    for gi, w in enumerate(POOL_WINDOWS):
        lanes = slice(gi * POOL_GC, (gi + 1) * POOL_GC)
        win = ext[:, lanes]
        k = 1
        while k < w:
            win = win + pltpu.roll(win, k, axis=0)
            k *= 2
        if seq_start is False:
            inv_cnt = 1.0 / w
        else:
            inv_cnt = 1.0 / jnp.where(seq_start, jnp.minimum(row + 1, w), w).astype(F32)
        p = win[POOL_HALO:] * inv_cnt - hp[:, lanes]
        ys.append(_dot(p.astype(BF16), pw_ref[0, gi]) + pb_ref[0, gi:gi + 1, :])
    y = jnp.concatenate(ys, axis=1) * ps_ref[0]
    return x + y, hp[rows - POOL_HALO:]


def _interleave(main, side):
    for _ in main:
        next(side, None)
    for _ in side:
        pass


def _pool_xattn_kernel(x_ref, xn_ref, gmix_ref, pw_ref, pb_ref, ps_ref, gx_ref, wq_ref, kv_ref,
                       wo_ref, o_ref, hna_ref, hnb_ref, x1a_ref, x1b_ref, *, tm, tiles_per_seq):
    t = pl.program_id(0)
    gmix = gmix_ref[0]

    def prepare(src_ref, src_row0, halo, seq_start, hn_ref, x1_ref):
        hprev = jnp.where(seq_start, 0.0, _rms(halo, gmix))
        for c in range(tm // PREP_ROWS):
            r0 = c * PREP_ROWS
            xt, hprev = _pool_mix(src_ref[src_row0 + r0:src_row0 + r0 + PREP_ROWS, :], hprev,
                                  seq_start if c == 0 else False, gmix, pw_ref, pb_ref, ps_ref)
            x1_ref[r0:r0 + PREP_ROWS, :] = xt
            hn_ref[r0:r0 + PREP_ROWS, :] = _rms(xt, gx_ref[0]).astype(BF16)
            yield

    def attend(row0, hn_ref, x1_ref):
        pair = 2 * X_DH
        outs = []
        for hp in range(X_HEADS // 2):
            cols = slice(hp * pair, (hp + 1) * pair)
            q = (_dot(hn_ref[...], wq_ref[0, :, cols]) * (X_DH ** -0.5)).astype(BF16)
            yield
            outs += _attend_heads(q, kv_ref, (2 * hp, 2 * hp + 1))
            yield
        o = jnp.concatenate(outs, axis=1)
        for hp in range(X_HEADS // 2):
            cols = slice(hp * pair, (hp + 1) * pair)
            o_ref[row0:row0 + tm, cols] = x1_ref[:, cols] + _dot(o, wo_ref[0, :, cols])
            yield

    @pl.when(t == 0)
    def _():
        for _ in prepare(x_ref, 0, x_ref[0:POOL_HALO, :], True, hna_ref, x1a_ref):
            pass

    _interleave(attend(0, hna_ref, x1a_ref),
                prepare(x_ref, tm, x_ref[tm - POOL_HALO:tm, :], False, hnb_ref, x1b_ref))
    next_opens_seq = lax.rem(2 * t + 2, tiles_per_seq) == 0
    _interleave(attend(tm, hnb_ref, x1b_ref),
                prepare(xn_ref, 0, x_ref[2 * tm - POOL_HALO:2 * tm, :], next_opens_seq,
                        hna_ref, x1a_ref))


def _pool_xattn(x, layer, pool_layer, kv, gmix, pw, pb, ps, gx, wq, wo):
    b, s, d = x.shape
    tm = POOL_TM
    n_tiles = b * s // tm
    tiles_per_seq = s // tm
    assert tiles_per_seq % 2 == 0
    step_spec = pl.BlockSpec((2 * tm, d), lambda t: (t, 0))
    x2d = x.reshape(b * s, d)
    out = pl.pallas_call(
        functools.partial(_pool_xattn_kernel, tm=tm, tiles_per_seq=tiles_per_seq),
        grid=(n_tiles // 2,),
        in_specs=[
            step_spec,
            pl.BlockSpec((tm, d), lambda t: (jnp.minimum(2 * t + 2, n_tiles - 1), 0)),
            _layer_block(gmix, layer),
            _layer_block(pw, pool_layer),
            _layer_block(pb, pool_layer),
            _layer_block(ps, pool_layer),
            _layer_block(gx, layer),
            _layer_block(wq, 0),
            pl.BlockSpec((1, 1, MEM_LEN, 2 * d),
                         lambda t: (layer, (2 * t) // tiles_per_seq, 0, 0)),
            _layer_block(wo, 0),
        ],
        out_specs=step_spec,
        out_shape=jax.ShapeDtypeStruct((b * s, d), F32),
        scratch_shapes=[pltpu.VMEM((tm, d), BF16), pltpu.VMEM((tm, d), BF16),
                        pltpu.VMEM((tm, d), F32), pltpu.VMEM((tm, d), F32)],
        compiler_params=_params(("arbitrary",)),
        name="pool_xattn",
    )(x2d, x2d, gmix, pw, pb, ps, gx, wq, kv, wo)
    return out.reshape(b, s, d)


def _gla_kernel(x_ref, gmix_ref, win_ref, wg1_ref, wg2_ref, bg_ref, gon_ref, wout_ref,
                o_ref, state_ref, *, tm):
    @pl.when(pl.program_id(1) == 0)
    def _():
        state_ref[...] = jnp.zeros_like(state_ref)

    nc = tm // CHUNK
    x = x_ref[0]
    h = _rms(x, gmix_ref[0]).astype(BF16)
    z = _dot(_dot(h, wg1_ref[0]).astype(BF16), wg2_ref[0]) + bg_ref[0]
    v = _dot(h, win_ref[0, :, 2 * GLA_KD:2 * GLA_KD + GLA_VD]).astype(BF16)
    r = _dot(h, win_ref[0, :, 2 * GLA_KD + GLA_VD:])
    glog = jax.nn.log_sigmoid(z) * (1.0 / GATE_TAU)
    g_hi = glog.astype(BF16)
    g_lo = (glog - g_hi.astype(F32)).astype(BF16)
    row = lax.broadcasted_iota(jnp.int32, (CHUNK, CHUNK), 0)
    col = lax.broadcasted_iota(jnp.int32, (CHUNK, CHUNK), 1)
    causal = row >= col
    tril = causal.astype(BF16)
    tril2 = jnp.concatenate([tril, tril], axis=1)
    chunk_rows = [slice(c * CHUNK, (c + 1) * CHUNK) for c in range(nc)]
    heads = [(slice(hh * GLA_DK, (hh + 1) * GLA_DK), slice(hh * GLA_DV, (hh + 1) * GLA_DV))
             for hh in range(GLA_HEADS)]

    b = jnp.concatenate(
        [_dot(tril2, jnp.concatenate([g_hi[rows], g_lo[rows]], axis=0)) for rows in chunk_rows],
        axis=0)
    b = b.reshape(nc, CHUNK, GLA_KD)
    b_ref = b[:, CHUNK // 2 - 1:CHUNK // 2, :]
    b_last = b[:, CHUNK - 1:CHUNK, :]
    qk = _dot(h, win_ref[0, :, 0:2 * GLA_KD])
    q = (qk[:, 0:GLA_KD] * (GLA_DK ** -0.5)).reshape(nc, CHUNK, GLA_KD)
    k = qk[:, GLA_KD:].reshape(nc, CHUNK, GLA_KD)
    qa = (q * jnp.exp(b - b_ref)).astype(BF16)
    kb = (k * jnp.exp(b_ref - b)).astype(BF16)
    qi = (q * jnp.exp(b)).astype(BF16)
    kst = k * jnp.exp(b_last - b)

    kst_t = [kst[c].T.astype(BF16) for c in range(nc)]
    decay = [jnp.exp(b[c, CHUNK - 8:CHUNK, :].T[:, 7:8]) for c in range(nc)]
    att = [[jnp.where(causal, _dot_nt(qa[c][:, kl], kb[c][:, kl]), 0.0).astype(BF16)
            for kl, _ in heads] for c in range(nc)]
    upd = [[_dot(kst_t[c][kl], v[chunk_rows[c], vl]) for kl, vl in heads] for c in range(nc)]

    o_rows = [[None] * GLA_HEADS for _ in range(nc)]
    for hh, (kl, vl) in enumerate(heads):
        st = state_ref[hh]
        for c in range(nc):
            lhs = jnp.concatenate([qi[c][:, kl], att[c][hh]], axis=1)
            rhs = jnp.concatenate([st.astype(BF16), v[chunk_rows[c], vl]], axis=0)
            o_rows[c][hh] = _dot(lhs, rhs)
            st = decay[c][kl] * st + upd[c][hh]
        state_ref[hh] = st
    o = jnp.concatenate([jnp.concatenate(o_rows[c], axis=1) for c in range(nc)], axis=0)

    gate = r * jax.nn.sigmoid(r)
    gon = gon_ref[0]
    outs = [(_rms(o[:, vl], gon) * gate[:, vl]).astype(BF16) for _, vl in heads]
    o_ref[0] = x + _dot(jnp.concatenate(outs, axis=1), wout_ref[0])


def _gla(x, layer, gla_layer, gmix, w_in, w_g1, w_g2, b_g, g_on, w_out):
    b, s, d = x.shape
    tm = GLA_TM
    x_spec = pl.BlockSpec((1, tm, d), lambda bi, i: (bi, i, 0))
    return pl.pallas_call(
        functools.partial(_gla_kernel, tm=tm),
        grid=(b, s // tm),
        in_specs=[
            x_spec,
            _layer_block(gmix, layer),
            _layer_block(w_in, 0),
            _layer_block(w_g1, gla_layer),
            _layer_block(w_g2, gla_layer),
            _layer_block(b_g, gla_layer),
            _layer_block(g_on, gla_layer),
            _layer_block(w_out, 0),
        ],
        out_specs=x_spec,
        out_shape=jax.ShapeDtypeStruct((b, s, d), F32),
        scratch_shapes=[pltpu.VMEM((GLA_HEADS, GLA_DK, GLA_DV), F32)],
        compiler_params=_params(("arbitrary", "arbitrary")),
        name="gla",
    )(x, gmix, w_in, w_g1, w_g2, b_g, g_on, w_out)


def _rows(p):
    return p.reshape(p.shape[0], 1, p.shape[1])


def _bf16_layer(w, layer):
    return w[layer:layer + 1].astype(BF16)


def kernel(x, mem, n_ffn1, ffn1_w_gu, ffn1_w_down, n_mix, gla_w_in, gla_w_g1, gla_w_g2,
           gla_b_g, gla_onorm, gla_w_out, pool_w, pool_b, pool_scale, n_xattn, n_mem,
           x_w_q, x_w_kv, x_w_o, n_ffn2, ffn2_w_gu, ffn2_w_down, n_final):
    n_ffn1, n_mix, n_xattn, n_mem, n_ffn2, gla_b_g, gla_onorm, pool_scale = (
        _rows(p) for p in (n_ffn1, n_mix, n_xattn, n_mem, n_ffn2, gla_b_g, gla_onorm, pool_scale))
    gla_w_g1, gla_w_g2, pool_w = (w.astype(BF16) for w in (gla_w_g1, gla_w_g2, pool_w))
    kv = _mem_kv(mem, n_mem, x_w_kv.astype(BF16))

    ffn_w = [_bf16_layer(ffn1_w_gu, 0), _bf16_layer(ffn1_w_down, 0)]
    gla_w = [_bf16_layer(gla_w_in, 0), _bf16_layer(gla_w_out, 0)]
    for i in range(DEPTH):
        x, (f2_gu, f2_down, wq, wo) = _ffn(
            x, i, n_ffn1, *ffn_w,
            casts=[(ffn2_w_gu, i), (ffn2_w_down, i), (x_w_q, i), (x_w_o, i)])
        j = i // 2
        if i % 2 == 0:
            x = _gla(x, i, j, n_mix, gla_w[0], gla_w_g1, gla_w_g2, gla_b_g, gla_onorm, gla_w[1])
            x = _xattn(x, i, kv, n_xattn, wq, wo)
        else:
            x = _pool_xattn(x, i, j, kv, n_mix, pool_w, pool_b, pool_scale, n_xattn, wq, wo)
        casts = []
        if i + 1 < DEPTH:
            casts = [(ffn1_w_gu, i + 1), (ffn1_w_down, i + 1)]
            if (i + 1) % 2 == 0:
                casts += [(gla_w_in, (i + 1) // 2), (gla_w_out, (i + 1) // 2)]
        g_final = n_final if i == DEPTH - 1 else None
        x, cast_outs = _ffn(x, i, n_ffn2, f2_gu, f2_down, g_final, casts=casts)
        ffn_w, gla_w = cast_outs[:2], cast_outs[2:]
    return x
```

```python
import functools

import jax
import jax.numpy as jnp
from jax import lax
from jax.experimental import pallas as pl
from jax.experimental.pallas import tpu as pltpu

F32 = jnp.float32
BF16 = jnp.bfloat16

D_MODEL = 1024
DEPTH = 4
EPS = 1e-6
D_FF = 2816
GLA_HEADS = 4
GLA_KD = 512
GLA_VD = 1024
GLA_DK = 128
GLA_DV = 256
GATE_TAU = 16.0
LOG2E = 1.4426950408889634
CHUNK = 64
POOL_WINDOWS = (2, 4, 8, 16)
POOL_GC = 256
POOL_HALO = 16
MEM_LEN = 256
X_HEADS = 4
X_DH = 256

MXU_TILE = 256
BF16_ROWS = 16
FFN_CHUNK = MXU_TILE
FFN_TM = 1024
KV_COLS = 2 * MXU_TILE
XATTN_TM = 1024
POOL_TM = 512
PREP_ROWS = 128
GLA_TM = 1024
VMEM_LIMIT = 56 * 1024 * 1024


def _rms(x, g):
    ms = jnp.mean(x * x, axis=-1, keepdims=True)
    return x * lax.rsqrt(ms + EPS) * g


def _layer_block(arr, layer):
    nd = arr.ndim
    return pl.BlockSpec((1,) + arr.shape[1:], lambda *_: (layer,) + (0,) * (nd - 1),
                        pipeline_mode=pl.Buffered(1))


def _dot(a, b):
    return jnp.dot(a, b, preferred_element_type=F32)


def _dot_nt(a, b):
    return lax.dot_general(a, b, (((1,), (1,)), ((), ())), preferred_element_type=F32)


def _params(semantics):
    return pltpu.CompilerParams(dimension_semantics=semantics, vmem_limit_bytes=VMEM_LIMIT)


def _cast_plan(w, layer, n_steps, step_of=lambda i: i):
    layers, rows, cols = w.shape
    rpb = next(k for k in range(BF16_ROWS, rows + 1, BF16_ROWS)
               if rows % k == 0 and rows // k <= n_steps)
    nblk = rows // rpb

    def blk(*idx):
        return jnp.minimum(step_of(*idx), nblk - 1)

    src = w.reshape(layers, nblk, rpb, cols)
    in_spec = pl.BlockSpec((1, 1, rpb, cols), lambda *idx: (layer, blk(*idx), 0, 0))
    out_spec = pl.BlockSpec((1, rpb, cols), lambda *idx: (blk(*idx), 0, 0))
    return src, in_spec, out_spec, jax.ShapeDtypeStruct((nblk, rpb, cols), BF16)


def _cast_jobs(casts, n_steps, step_of=lambda i: i):
    plans = [_cast_plan(w, layer, n_steps, step_of) for w, layer in casts]
    return tuple(list(col) for col in zip(*plans)) if plans else ([], [], [], [])


def _run_casts(src_refs, dst_refs):
    for src_ref, dst_ref in zip(src_refs, dst_refs):
        dst_ref[0] = src_ref[0, 0].astype(BF16)


def _cast_results(outs, casts):
    return [o.reshape(1, w.shape[1], w.shape[2]) for o, (w, _) in zip(outs, casts)]


def _ffn_kernel(*refs, final_norm, n_casts):
    x_ref, g_ref, wgu_ref, wd_ref = refs[:4]
    rest = refs[4:]
    if final_norm:
        gf_ref, rest = rest[0], rest[1:]
    cast_src, o_ref, cast_dst = rest[:n_casts], rest[n_casts], rest[n_casts + 1:]
    _run_casts(cast_src, cast_dst)
    x = x_ref[...]
    h = _rms(x, g_ref[0]).astype(BF16)
    acc = None
    for c in range(D_FF // FFN_CHUNK):
        cols = slice(c * FFN_CHUNK, (c + 1) * FFN_CHUNK)
        g = _dot(h, wgu_ref[0, :, cols])
        u = _dot(h, wgu_ref[0, :, D_FF + c * FFN_CHUNK:D_FF + (c + 1) * FFN_CHUNK])
        a = (g * jax.nn.sigmoid(g) * u).astype(BF16)
        d = _dot(a, wd_ref[0, cols, :])
        acc = d if acc is None else acc + d
    y = x + 0.5 * acc
    if final_norm:
        y = _rms(y, gf_ref[...])
    o_ref[...] = y


def _ffn(x, layer, g, w_gu, w_down, g_final=None, casts=()):
    b, s, d = x.shape
    t = b * s
    n_steps = t // FFN_TM
    final_norm = g_final is not None
    row_spec = pl.BlockSpec((FFN_TM, d), lambda i: (i, 0))
    in_specs = [row_spec, _layer_block(g, layer), _layer_block(w_gu, 0), _layer_block(w_down, 0)]
    args = [x.reshape(t, d), g, w_gu, w_down]
    if final_norm:
        in_specs.append(pl.BlockSpec((1, d), lambda i: (0, 0), pipeline_mode=pl.Buffered(1)))
        args.append(g_final.reshape(1, d))
    c_args, c_in, c_out, c_shapes = _cast_jobs(casts, n_steps)
    outs = pl.pallas_call(
        functools.partial(_ffn_kernel, final_norm=final_norm, n_casts=len(casts)),
        grid=(n_steps,),
        in_specs=in_specs + c_in,
        out_specs=[row_spec] + c_out,
        out_shape=[jax.ShapeDtypeStruct((t, d), F32)] + c_shapes,
        compiler_params=_params(("arbitrary",)),
        name="ffn",
    )(*args, *c_args)
    return outs[0].reshape(b, s, d), _cast_results(outs[1:], casts)


def _kv_kernel(mem_ref, g_ref, w_ref, *rest, n_casts):
    cast_src, o_ref, cast_dst, mn_ref = (rest[:n_casts], rest[n_casts],
                                         rest[n_casts + 1:-1], rest[-1])
    _run_casts(cast_src, cast_dst)

    @pl.when(pl.program_id(1) == 0)
    def _():
        mn_ref[...] = _rms(mem_ref[...], g_ref[0]).astype(BF16)

    o_ref[0] = _dot(mn_ref[...], w_ref[0].astype(BF16)).astype(BF16)


def _mem_kv(mem, n_mem, w_kv, casts=()):
    b, m, d = mem.shape
    depth = w_kv.shape[0]
    n_col = 2 * d // KV_COLS
    c_args, c_in, c_out, c_shapes = _cast_jobs(casts, depth * n_col, lambda l, j: l * n_col + j)
    outs = pl.pallas_call(
        functools.partial(_kv_kernel, n_casts=len(casts)),
        grid=(depth, n_col),
        in_specs=[
            pl.BlockSpec((b * m, d), lambda l, j: (0, 0), pipeline_mode=pl.Buffered(1)),
            pl.BlockSpec((1, 1, d), lambda l, j: (l, 0, 0)),
            pl.BlockSpec((1, d, KV_COLS), lambda l, j: (l, 0, j)),
        ] + c_in,
        out_specs=[pl.BlockSpec((1, b * m, KV_COLS), lambda l, j: (l, 0, j))] + c_out,
        out_shape=[jax.ShapeDtypeStruct((depth, b * m, 2 * d), BF16)] + c_shapes,
        scratch_shapes=[pltpu.VMEM((b * m, d), BF16)],
        compiler_params=_params(("arbitrary", "arbitrary")),
        name="mem_kv",
    )(mem.reshape(b * m, d), n_mem, w_kv, *c_args)
    return outs[0].reshape(depth, b, m, 2 * d), _cast_results(outs[1:], casts)


def _attend_heads(q, kv_ref, heads):
    outs = []
    for n, h in enumerate(heads):
        kh = kv_ref[0, 0, :, h * X_DH:(h + 1) * X_DH]
        vh = kv_ref[0, 0, :, D_MODEL + h * X_DH:D_MODEL + (h + 1) * X_DH]
        s = _dot_nt(q[:, n * X_DH:(n + 1) * X_DH], kh)
        e = jnp.exp(s - jnp.max(s, axis=-1, keepdims=True))
        p = e * (1.0 / jnp.sum(e, axis=-1, keepdims=True))
        outs.append(_dot(p.astype(BF16), vh).astype(BF16))
    return outs


def _xattn_kernel(x_ref, gx_ref, wq_ref, kv_ref, wo_ref, o_ref):
    x = x_ref[0]
    hn = _rms(x, gx_ref[0]).astype(BF16)
    q = (_dot(hn, wq_ref[0]) * (X_DH ** -0.5)).astype(BF16)
    o = jnp.concatenate(_attend_heads(q, kv_ref, range(X_HEADS)), axis=1)
    o_ref[0] = x + _dot(o, wo_ref[0])


def _xattn(x, layer, kv, gx, wq, wo):
    b, s, d = x.shape
    tm = XATTN_TM
    x_spec = pl.BlockSpec((1, tm, d), lambda bi, i: (bi, i, 0))
    return pl.pallas_call(
        _xattn_kernel,
        grid=(b, s // tm),
        in_specs=[
            x_spec,
            _layer_block(gx, layer),
            _layer_block(wq, 0),
            pl.BlockSpec((1, 1, MEM_LEN, 2 * d), lambda bi, i: (layer, bi, 0, 0)),
            _layer_block(wo, 0),
        ],
        out_specs=x_spec,
        out_shape=jax.ShapeDtypeStruct((b, s, d), F32),
        compiler_params=_params(("arbitrary", "arbitrary")),
        name="xattn",
    )(x, gx, wq, kv, wo)


def _pool_mix(x, hprev, seq_start, gmix, pw_ref, pb_ref, ps_ref):
    rows = x.shape[0]
    hp = _rms(x, gmix)
    ext = jnp.concatenate([hprev, hp], axis=0)
    row = lax.broadcasted_iota(jnp.int32, (rows, 1), 0)
    ys = []
    for gi, w in enumerate(POOL_WINDOWS):
        lanes = slice(gi * POOL_GC, (gi + 1) * POOL_GC)
        win = ext[:, lanes]
        k = 1
        while k < w:
            win = win + pltpu.roll(win, k, axis=0)
            k *= 2
        if seq_start is False:
            inv_cnt = 1.0 / w
        else:
            inv_cnt = 1.0 / jnp.where(seq_start, jnp.minimum(row + 1, w), w).astype(F32)
        p = win[POOL_HALO:] * inv_cnt - hp[:, lanes]
        ys.append(_dot(p.astype(BF16), pw_ref[0, gi]) + pb_ref[0, gi:gi + 1, :])
    y = jnp.concatenate(ys, axis=1) * ps_ref[0]
    return x + y, hp[rows - POOL_HALO:]


def _interleave(main, side):
    for _ in main:
        next(side, None)
    for _ in side:
        pass


def _pool_xattn_kernel(x_ref, xn_ref, gmix_ref, pw_ref, pb_ref, ps_ref, gx_ref, wq_ref, kv_ref,
                       wo_ref, o_ref, hna_ref, hnb_ref, x1a_ref, x1b_ref, *, tm, tiles_per_seq):
    t = pl.program_id(0)
    gmix = gmix_ref[0]

    def prepare(src_ref, src_row0, halo, seq_start, hn_ref, x1_ref):
        hprev = jnp.where(seq_start, 0.0, _rms(halo, gmix))
        for c in range(tm // PREP_ROWS):
            r0 = c * PREP_ROWS
            xt, hprev = _pool_mix(src_ref[src_row0 + r0:src_row0 + r0 + PREP_ROWS, :], hprev,
                                  seq_start if c == 0 else False, gmix, pw_ref, pb_ref, ps_ref)
            x1_ref[r0:r0 + PREP_ROWS, :] = xt
            hn_ref[r0:r0 + PREP_ROWS, :] = _rms(xt, gx_ref[0]).astype(BF16)
            yield

    def attend(row0, hn_ref, x1_ref):
        pair = 2 * X_DH
        outs = []
        for hp in range(X_HEADS // 2):
            cols = slice(hp * pair, (hp + 1) * pair)
            q = (_dot(hn_ref[...], wq_ref[0, :, cols]) * (X_DH ** -0.5)).astype(BF16)
            yield
            outs += _attend_heads(q, kv_ref, (2 * hp, 2 * hp + 1))
            yield
        o = jnp.concatenate(outs, axis=1)
        for hp in range(X_HEADS // 2):
            cols = slice(hp * pair, (hp + 1) * pair)
            o_ref[row0:row0 + tm, cols] = x1_ref[:, cols] + _dot(o, wo_ref[0, :, cols])
            yield

    @pl.when(t == 0)
    def _():
        for _ in prepare(x_ref, 0, x_ref[0:POOL_HALO, :], True, hna_ref, x1a_ref):
            pass

    _interleave(attend(0, hna_ref, x1a_ref),
                prepare(x_ref, tm, x_ref[tm - POOL_HALO:tm, :], False, hnb_ref, x1b_ref))
    next_opens_seq = lax.rem(2 * t + 2, tiles_per_seq) == 0
    _interleave(attend(tm, hnb_ref, x1b_ref),
                prepare(xn_ref, 0, x_ref[2 * tm - POOL_HALO:2 * tm, :], next_opens_seq,
                        hna_ref, x1a_ref))


def _pool_xattn(x, layer, pool_layer, kv, gmix, pw, pb, ps, gx, wq, wo):
    b, s, d = x.shape
    tm = POOL_TM
    n_tiles = b * s // tm
    tiles_per_seq = s // tm
    assert tiles_per_seq % 2 == 0
    step_spec = pl.BlockSpec((2 * tm, d), lambda t: (t, 0))
    x2d = x.reshape(b * s, d)
    out = pl.pallas_call(
        functools.partial(_pool_xattn_kernel, tm=tm, tiles_per_seq=tiles_per_seq),
        grid=(n_tiles // 2,),
        in_specs=[
            step_spec,
            pl.BlockSpec((tm, d), lambda t: (jnp.minimum(2 * t + 2, n_tiles - 1), 0)),
            _layer_block(gmix, layer),
            _layer_block(pw, pool_layer),
            _layer_block(pb, pool_layer),
            _layer_block(ps, pool_layer),
            _layer_block(gx, layer),
            _layer_block(wq, 0),
            pl.BlockSpec((1, 1, MEM_LEN, 2 * d),
                         lambda t: (layer, (2 * t) // tiles_per_seq, 0, 0)),
            _layer_block(wo, 0),
        ],
        out_specs=step_spec,
        out_shape=jax.ShapeDtypeStruct((b * s, d), F32),
        scratch_shapes=[pltpu.VMEM((tm, d), BF16), pltpu.VMEM((tm, d), BF16),
                        pltpu.VMEM((tm, d), F32), pltpu.VMEM((tm, d), F32)],
        compiler_params=_params(("arbitrary",)),
        name="pool_xattn",
    )(x2d, x2d, gmix, pw, pb, ps, gx, wq, kv, wo)
    return out.reshape(b, s, d)


def _gla_kernel(x_ref, gmix_ref, win_ref, wg1_ref, wg2_ref, bg_ref, gon_ref, wout_ref,
                o_ref, state_ref, *, tm):
    @pl.when(pl.program_id(1) == 0)
    def _():
        state_ref[...] = jnp.zeros_like(state_ref)

    nc = tm // CHUNK
    x = x_ref[0]
    h = _rms(x, gmix_ref[0]).astype(BF16)
    gate_in = jnp.concatenate([_dot(h[:tm // 2], wg1_ref[0]), _dot(h[tm // 2:], wg1_ref[0])], axis=0)
    z = _dot(gate_in.astype(BF16), wg2_ref[0]) + bg_ref[0]
    v = _dot(h, win_ref[0, :, 2 * GLA_KD:2 * GLA_KD + GLA_VD]).astype(BF16)
    r = _dot(h, win_ref[0, :, 2 * GLA_KD + GLA_VD:])
    zs = z * LOG2E
    glog = (jnp.minimum(zs, 0.0) - jnp.log2(1.0 + jnp.exp2(-jnp.abs(zs)))) * (1.0 / GATE_TAU)
    g_hi = glog.astype(BF16)
    g_lo = (glog - g_hi.astype(F32)).astype(BF16)
    row = lax.broadcasted_iota(jnp.int32, (CHUNK, CHUNK), 0)
    col = lax.broadcasted_iota(jnp.int32, (CHUNK, CHUNK), 1)
    causal = row >= col
    tril = causal.astype(BF16)
    tril2 = jnp.concatenate([tril, tril], axis=1)
    chunk_rows = [slice(c * CHUNK, (c + 1) * CHUNK) for c in range(nc)]
    heads = [(slice(hh * GLA_DK, (hh + 1) * GLA_DK), slice(hh * GLA_DV, (hh + 1) * GLA_DV))
             for hh in range(GLA_HEADS)]

    b = jnp.concatenate(
        [_dot(tril2, jnp.concatenate([g_hi[rows], g_lo[rows]], axis=0)) for rows in chunk_rows],
        axis=0)
    b = b.reshape(nc, CHUNK, GLA_KD)
    b_ref = b[:, CHUNK // 2 - 1:CHUNK // 2, :]
    b_last = b[:, CHUNK - 1:CHUNK, :]
    qk = _dot(h, win_ref[0, :, 0:2 * GLA_KD])
    q = (qk[:, 0:GLA_KD] * (GLA_DK ** -0.5)).reshape(nc, CHUNK, GLA_KD)
    k = qk[:, GLA_KD:].reshape(nc, CHUNK, GLA_KD)
    qa = (q * jnp.exp2(b - b_ref)).astype(BF16)
    kb = (k * jnp.exp2(b_ref - b)).astype(BF16)
    qi = (q * jnp.exp2(b)).astype(BF16)
    kst = k * jnp.exp2(b_last - b)

    kst_t = [kst[c].T.astype(BF16) for c in range(nc)]
    decay = [jnp.exp2(b[c, CHUNK - 8:CHUNK, :].T[:, 7:8]) for c in range(nc)]
    att = [[jnp.where(causal, _dot_nt(qa[c][:, kl], kb[c][:, kl]), 0.0).astype(BF16)
            for kl, _ in heads] for c in range(nc)]
    upd = [[_dot(kst_t[c][kl], v[chunk_rows[c], vl]) for kl, vl in heads] for c in range(nc)]

    o_rows = [[None] * GLA_HEADS for _ in range(nc)]
    for hh, (kl, vl) in enumerate(heads):
        st = state_ref[hh]
        for c in range(nc):
            lhs = jnp.concatenate([qi[c][:, kl], att[c][hh]], axis=1)
            rhs = jnp.concatenate([st.astype(BF16), v[chunk_rows[c], vl]], axis=0)
            o_rows[c][hh] = _dot(lhs, rhs)
            st = decay[c][kl] * st + upd[c][hh]
        state_ref[hh] = st
    o = jnp.concatenate([jnp.concatenate(o_rows[c], axis=1) for c in range(nc)], axis=0)

    gate = r * jax.nn.sigmoid(r)
    gon = gon_ref[0]
    outs = [(_rms(o[:, vl], gon) * gate[:, vl]).astype(BF16) for _, vl in heads]
    o_ref[0] = x + _dot(jnp.concatenate(outs, axis=1), wout_ref[0])


def _gla(x, layer, gla_layer, gmix, w_in, w_g1, w_g2, b_g, g_on, w_out):
    b, s, d = x.shape
    tm = GLA_TM
    x_spec = pl.BlockSpec((1, tm, d), lambda bi, i: (bi, i, 0))
    return pl.pallas_call(
        functools.partial(_gla_kernel, tm=tm),
        grid=(b, s // tm),
        in_specs=[
            x_spec,
            _layer_block(gmix, layer),
            _layer_block(w_in, 0),
            _layer_block(w_g1, gla_layer),
            _layer_block(w_g2, gla_layer),
            _layer_block(b_g, gla_layer),
            _layer_block(g_on, gla_layer),
            _layer_block(w_out, 0),
        ],
        out_specs=x_spec,
        out_shape=jax.ShapeDtypeStruct((b, s, d), F32),
        scratch_shapes=[pltpu.VMEM((GLA_HEADS, GLA_DK, GLA_DV), F32)],
        compiler_params=_params(("arbitrary", "arbitrary")),
        name="gla",
    )(x, gmix, w_in, w_g1, w_g2, b_g, g_on, w_out)


def _rows(p):
    return p.reshape(p.shape[0], 1, p.shape[1])


def kernel(x, mem, n_ffn1, ffn1_w_gu, ffn1_w_down, n_mix, gla_w_in, gla_w_g1, gla_w_g2,
           gla_b_g, gla_onorm, gla_w_out, pool_w, pool_b, pool_scale, n_xattn, n_mem,
           x_w_q, x_w_kv, x_w_o, n_ffn2, ffn2_w_gu, ffn2_w_down, n_final):
    n_ffn1, n_mix, n_xattn, n_mem, n_ffn2, gla_b_g, gla_onorm, pool_scale = (
        _rows(p) for p in (n_ffn1, n_mix, n_xattn, n_mem, n_ffn2, gla_b_g, gla_onorm, pool_scale))
    gla_w_g1, gla_w_g2, pool_w = (w.astype(BF16) for w in (gla_w_g1, gla_w_g2, pool_w))
    kv, first_w = _mem_kv(mem, n_mem, x_w_kv,
                          casts=[(ffn1_w_gu, 0), (ffn1_w_down, 0), (gla_w_in, 0), (gla_w_out, 0)])
    ffn_w, gla_w = first_w[:2], first_w[2:]
    for i in range(DEPTH):
        x, (f2_gu, f2_down, wq, wo) = _ffn(
            x, i, n_ffn1, *ffn_w,
            casts=[(ffn2_w_gu, i), (ffn2_w_down, i), (x_w_q, i), (x_w_o, i)])
        j = i // 2
        if i % 2 == 0:
            x = _gla(x, i, j, n_mix, gla_w[0], gla_w_g1, gla_w_g2, gla_b_g, gla_onorm, gla_w[1])
            x = _xattn(x, i, kv, n_xattn, wq, wo)
        else:
            x = _pool_xattn(x, i, j, kv, n_mix, pool_w, pool_b, pool_scale, n_xattn, wq, wo)
        casts = []
        if i + 1 < DEPTH:
            casts = [(ffn1_w_gu, i + 1), (ffn1_w_down, i + 1)]
            if (i + 1) % 2 == 0:
                casts += [(gla_w_in, (i + 1) // 2), (gla_w_out, (i + 1) // 2)]
        g_final = n_final if i == DEPTH - 1 else None
        x, cast_outs = _ffn(x, i, n_ffn2, f2_gu, f2_down, g_final, casts=casts)
        ffn_w, gla_w = cast_outs[:2], cast_outs[2:]
    return x
```

```python
import functools

import jax
import jax.numpy as jnp
from jax import lax
from jax.experimental import pallas as pl
from jax.experimental.pallas import tpu as pltpu

F32 = jnp.float32
BF16 = jnp.bfloat16

D_MODEL = 1024
DEPTH = 4
EPS = 1e-6
D_FF = 2816
GLA_HEADS = 4
GLA_KD = 512
GLA_VD = 1024
GLA_DK = 128
GLA_DV = 256
GATE_TAU = 16.0
LOG2E = 1.4426950408889634
CHUNK = 64
POOL_WINDOWS = (2, 4, 8, 16)
POOL_GC = 256
POOL_HALO = 16
MEM_LEN = 256
X_HEADS = 4
X_DH = 256

MXU_TILE = 256
BF16_ROWS = 16
FFN_CHUNK = MXU_TILE
FFN_TM = 1024
KV_COLS = 2 * MXU_TILE
XATTN_TM = 1024
POOL_TM = 512
PREP_ROWS = 128
GLA_TM = 1024
VMEM_LIMIT = 56 * 1024 * 1024


def _rms(x, g):
    ms = jnp.mean(x * x, axis=-1, keepdims=True)
    return x * lax.rsqrt(ms + EPS) * g


def _layer_block(arr, layer):
    nd = arr.ndim
    return pl.BlockSpec((1,) + arr.shape[1:], lambda *_: (layer,) + (0,) * (nd - 1),
                        pipeline_mode=pl.Buffered(1))


def _dot(a, b):
    return jnp.dot(a, b, preferred_element_type=F32)


def _dot_nt(a, b):
    return lax.dot_general(a, b, (((1,), (1,)), ((), ())), preferred_element_type=F32)


def _params(semantics):
    return pltpu.CompilerParams(dimension_semantics=semantics, vmem_limit_bytes=VMEM_LIMIT)


def _cast_plan(w, layer, n_steps, step_of=lambda i: i):
    layers, rows, cols = w.shape
    rpb = next(k for k in range(BF16_ROWS, rows + 1, BF16_ROWS)
               if rows % k == 0 and rows // k <= n_steps)
    nblk = rows // rpb

    def blk(*idx):
        return jnp.minimum(step_of(*idx), nblk - 1)

    src = w.reshape(layers, nblk, rpb, cols)
    in_spec = pl.BlockSpec((1, 1, rpb, cols), lambda *idx: (layer, blk(*idx), 0, 0))
    out_spec = pl.BlockSpec((1, rpb, cols), lambda *idx: (blk(*idx), 0, 0))
    return src, in_spec, out_spec, jax.ShapeDtypeStruct((nblk, rpb, cols), BF16)


def _cast_jobs(casts, n_steps, step_of=lambda i: i):
    plans = [_cast_plan(w, layer, n_steps, step_of) for w, layer in casts]
    return tuple(list(col) for col in zip(*plans)) if plans else ([], [], [], [])


def _run_casts(src_refs, dst_refs):
    for src_ref, dst_ref in zip(src_refs, dst_refs):
        dst_ref[0] = src_ref[0, 0].astype(BF16)


def _cast_results(outs, casts):
    return [o.reshape(1, w.shape[1], w.shape[2]) for o, (w, _) in zip(outs, casts)]


def _ffn_kernel(*refs, final_norm, n_casts):
    x_ref, g_ref, wgu_ref, wd_ref = refs[:4]
    rest = refs[4:]
    if final_norm:
        gf_ref, rest = rest[0], rest[1:]
    cast_src, o_ref, cast_dst = rest[:n_casts], rest[n_casts], rest[n_casts + 1:]
    _run_casts(cast_src, cast_dst)
    x = x_ref[...]
    h = _rms(x, g_ref[0]).astype(BF16)
    acc = None
    for c in range(D_FF // FFN_CHUNK):
        cols = slice(c * FFN_CHUNK, (c + 1) * FFN_CHUNK)
        g = _dot(h, wgu_ref[0, :, cols])
        u = _dot(h, wgu_ref[0, :, D_FF + c * FFN_CHUNK:D_FF + (c + 1) * FFN_CHUNK])
        a = (g * jax.nn.sigmoid(g) * u).astype(BF16)
        d = _dot(a, wd_ref[0, cols, :])
        acc = d if acc is None else acc + d
    y = x + 0.5 * acc
    if final_norm:
        y = _rms(y, gf_ref[...])
    o_ref[...] = y


def _ffn(x, layer, g, w_gu, w_down, g_final=None, casts=()):
    b, s, d = x.shape
    t = b * s
    n_steps = t // FFN_TM
    final_norm = g_final is not None
    row_spec = pl.BlockSpec((FFN_TM, d), lambda i: (i, 0))
    in_specs = [row_spec, _layer_block(g, layer), _layer_block(w_gu, 0), _layer_block(w_down, 0)]
    args = [x.reshape(t, d), g, w_gu, w_down]
    if final_norm:
        in_specs.append(pl.BlockSpec((1, d), lambda i: (0, 0), pipeline_mode=pl.Buffered(1)))
        args.append(g_final.reshape(1, d))
    c_args, c_in, c_out, c_shapes = _cast_jobs(casts, n_steps)
    outs = pl.pallas_call(
        functools.partial(_ffn_kernel, final_norm=final_norm, n_casts=len(casts)),
        grid=(n_steps,),
        in_specs=in_specs + c_in,
        out_specs=[row_spec] + c_out,
        out_shape=[jax.ShapeDtypeStruct((t, d), F32)] + c_shapes,
        compiler_params=_params(("arbitrary",)),
        name="ffn",
    )(*args, *c_args)
    return outs[0].reshape(b, s, d), _cast_results(outs[1:], casts)


def _kv_kernel(mem_ref, g_ref, w_ref, *rest, n_casts):
    cast_src, o_ref, cast_dst, mn_ref = (rest[:n_casts], rest[n_casts],
                                         rest[n_casts + 1:-1], rest[-1])
    _run_casts(cast_src, cast_dst)

    @pl.when(pl.program_id(1) == 0)
    def _():
        mn_ref[...] = _rms(mem_ref[...], g_ref[0]).astype(BF16)

    o_ref[0] = _dot(mn_ref[...], w_ref[0].astype(BF16)).astype(BF16)


def _mem_kv(mem, n_mem, w_kv, casts=()):
    b, m, d = mem.shape
    depth = w_kv.shape[0]
    n_col = 2 * d // KV_COLS
    c_args, c_in, c_out, c_shapes = _cast_jobs(casts, depth * n_col, lambda l, j: l * n_col + j)
    outs = pl.pallas_call(
        functools.partial(_kv_kernel, n_casts=len(casts)),
        grid=(depth, n_col),
        in_specs=[
            pl.BlockSpec((b * m, d), lambda l, j: (0, 0), pipeline_mode=pl.Buffered(1)),
            pl.BlockSpec((1, 1, d), lambda l, j: (l, 0, 0)),
            pl.BlockSpec((1, d, KV_COLS), lambda l, j: (l, 0, j)),
        ] + c_in,
        out_specs=[pl.BlockSpec((1, b * m, KV_COLS), lambda l, j: (l, 0, j))] + c_out,
        out_shape=[jax.ShapeDtypeStruct((depth, b * m, 2 * d), BF16)] + c_shapes,
        scratch_shapes=[pltpu.VMEM((b * m, d), BF16)],
        compiler_params=_params(("arbitrary", "arbitrary")),
        name="mem_kv",
    )(mem.reshape(b * m, d), n_mem, w_kv, *c_args)
    return outs[0].reshape(depth, b, m, 2 * d), _cast_results(outs[1:], casts)


def _attend_stages(hn, x1, wq_ref, kv_ref, wo_ref, store):
    q = (_dot(hn, wq_ref[0]) * (X_DH ** -0.5)).astype(BF16)
    yield
    heads = [slice(h * X_DH, (h + 1) * X_DH) for h in range(X_HEADS)]
    scores = [_dot_nt(q[:, cols], kv_ref[0, 0, :, cols]) for cols in heads]
    yield
    probs = []
    for s in scores:
        e = jnp.exp(s - jnp.max(s, axis=-1, keepdims=True))
        probs.append((e * (1.0 / jnp.sum(e, axis=-1, keepdims=True))).astype(BF16))
    yield
    o = jnp.concatenate(
        [_dot(p, kv_ref[0, 0, :, D_MODEL + h * X_DH:D_MODEL + (h + 1) * X_DH]).astype(BF16)
         for h, p in enumerate(probs)], axis=1)
    yield
    store(x1 + _dot(o, wo_ref[0]))
    yield


def _xattn_kernel(x_ref, gx_ref, wq_ref, kv_ref, wo_ref, o_ref):
    x = x_ref[0]
    hn = _rms(x, gx_ref[0]).astype(BF16)

    def store(rows):
        o_ref[0] = rows

    for _ in _attend_stages(hn, x, wq_ref, kv_ref, wo_ref, store):
        pass


def _xattn(x, layer, kv, gx, wq, wo):
    b, s, d = x.shape
    tm = XATTN_TM
    x_spec = pl.BlockSpec((1, tm, d), lambda bi, i: (bi, i, 0))
    return pl.pallas_call(
        _xattn_kernel,
        grid=(b, s // tm),
        in_specs=[
            x_spec,
            _layer_block(gx, layer),
            _layer_block(wq, 0),
            pl.BlockSpec((1, 1, MEM_LEN, 2 * d), lambda bi, i: (layer, bi, 0, 0)),
            _layer_block(wo, 0),
        ],
        out_specs=x_spec,
        out_shape=jax.ShapeDtypeStruct((b, s, d), F32),
        compiler_params=_params(("arbitrary", "arbitrary")),
        name="xattn",
    )(x, gx, wq, kv, wo)


def _pool_mix(x, hprev, seq_start, gmix, pw_ref, pb_ref, ps_ref):
    rows = x.shape[0]
    hp = _rms(x, gmix)
    ext = jnp.concatenate([hprev, hp], axis=0)
    row = lax.broadcasted_iota(jnp.int32, (rows, 1), 0)
    ys = []
    for gi, w in enumerate(POOL_WINDOWS):
        lanes = slice(gi * POOL_GC, (gi + 1) * POOL_GC)
        win = ext[:, lanes]
        k = 1
        while k < w:
            win = win + pltpu.roll(win, k, axis=0)
            k *= 2
        if seq_start is False:
            inv_cnt = 1.0 / w
        else:
            inv_cnt = 1.0 / jnp.where(seq_start, jnp.minimum(row + 1, w), w).astype(F32)
        p = win[POOL_HALO:] * inv_cnt - hp[:, lanes]
        ys.append(_dot(p.astype(BF16), pw_ref[0, gi]) + pb_ref[0, gi:gi + 1, :])
    y = jnp.concatenate(ys, axis=1) * ps_ref[0]
    return x + y, hp[rows - POOL_HALO:]


def _interleave(main, side):
    for _ in main:
        next(side, None)
    for _ in side:
        pass


def _pool_xattn_kernel(x_ref, xn_ref, gmix_ref, pw_ref, pb_ref, ps_ref, gx_ref, wq_ref, kv_ref,
                       wo_ref, o_ref, hna_ref, hnb_ref, x1a_ref, x1b_ref, *, tm, tiles_per_seq):
    t = pl.program_id(0)
    gmix = gmix_ref[0]

    def prepare(src_ref, src_row0, halo, seq_start, hn_ref, x1_ref):
        hprev = jnp.where(seq_start, 0.0, _rms(halo, gmix))
        for c in range(tm // PREP_ROWS):
            r0 = c * PREP_ROWS
            xt, hprev = _pool_mix(src_ref[src_row0 + r0:src_row0 + r0 + PREP_ROWS, :], hprev,
                                  seq_start if c == 0 else False, gmix, pw_ref, pb_ref, ps_ref)
            x1_ref[r0:r0 + PREP_ROWS, :] = xt
            hn_ref[r0:r0 + PREP_ROWS, :] = _rms(xt, gx_ref[0]).astype(BF16)
            yield

    def attend(row0, hn_ref, x1_ref):
        def store(rows):
            o_ref[row0:row0 + tm, :] = rows

        return _attend_stages(hn_ref[...], x1_ref[...], wq_ref, kv_ref, wo_ref, store)

    @pl.when(t == 0)
    def _():
        for _ in prepare(x_ref, 0, x_ref[0:POOL_HALO, :], True, hna_ref, x1a_ref):
            pass

    _interleave(attend(0, hna_ref, x1a_ref),
                prepare(x_ref, tm, x_ref[tm - POOL_HALO:tm, :], False, hnb_ref, x1b_ref))
    next_opens_seq = lax.rem(2 * t + 2, tiles_per_seq) == 0
    _interleave(attend(tm, hnb_ref, x1b_ref),
                prepare(xn_ref, 0, x_ref[2 * tm - POOL_HALO:2 * tm, :], next_opens_seq,
                        hna_ref, x1a_ref))


def _pool_xattn(x, layer, pool_layer, kv, gmix, pw, pb, ps, gx, wq, wo):
    b, s, d = x.shape
    tm = POOL_TM
    n_tiles = b * s // tm
    tiles_per_seq = s // tm
    assert tiles_per_seq % 2 == 0
    step_spec = pl.BlockSpec((2 * tm, d), lambda t: (t, 0))
    x2d = x.reshape(b * s, d)
    out = pl.pallas_call(
        functools.partial(_pool_xattn_kernel, tm=tm, tiles_per_seq=tiles_per_seq),
        grid=(n_tiles // 2,),
        in_specs=[
            step_spec,
            pl.BlockSpec((tm, d), lambda t: (jnp.minimum(2 * t + 2, n_tiles - 1), 0)),
            _layer_block(gmix, layer),
            _layer_block(pw, pool_layer),
            _layer_block(pb, pool_layer),
            _layer_block(ps, pool_layer),
            _layer_block(gx, layer),
            _layer_block(wq, 0),
            pl.BlockSpec((1, 1, MEM_LEN, 2 * d),
                         lambda t: (layer, (2 * t) // tiles_per_seq, 0, 0)),
            _layer_block(wo, 0),
        ],
        out_specs=step_spec,
        out_shape=jax.ShapeDtypeStruct((b * s, d), F32),
        scratch_shapes=[pltpu.VMEM((tm, d), BF16), pltpu.VMEM((tm, d), BF16),
                        pltpu.VMEM((tm, d), F32), pltpu.VMEM((tm, d), F32)],
        compiler_params=_params(("arbitrary",)),
        name="pool_xattn",
    )(x2d, x2d, gmix, pw, pb, ps, gx, wq, kv, wo)
    return out.reshape(b, s, d)


def _gla_kernel(x_ref, gmix_ref, win_ref, wg1_ref, wg2_ref, bg_ref, gon_ref, wout_ref,
                o_ref, state_ref, *, tm):
    @pl.when(pl.program_id(1) == 0)
    def _():
        state_ref[...] = jnp.zeros_like(state_ref)

    nc = tm // CHUNK
    x = x_ref[0]
    h = _rms(x, gmix_ref[0]).astype(BF16)
    gate_in = jnp.concatenate([_dot(h[:tm // 2], wg1_ref[0]), _dot(h[tm // 2:], wg1_ref[0])], axis=0)
    z = _dot(gate_in.astype(BF16), wg2_ref[0]) + bg_ref[0]
    qk = _dot(h, win_ref[0, :, 0:2 * GLA_KD])
    zs = z * LOG2E
    glog = (jnp.minimum(zs, 0.0) - jnp.log2(1.0 + jnp.exp2(-jnp.abs(zs)))) * (1.0 / GATE_TAU)
    g_hi = glog.astype(BF16)
    g_lo = (glog - g_hi.astype(F32)).astype(BF16)
    row = lax.broadcasted_iota(jnp.int32, (CHUNK, CHUNK), 0)
    col = lax.broadcasted_iota(jnp.int32, (CHUNK, CHUNK), 1)
    causal = row >= col
    tril = causal.astype(BF16)
    tril2 = jnp.concatenate([tril, tril], axis=1)
    chunk_rows = [slice(c * CHUNK, (c + 1) * CHUNK) for c in range(nc)]
    heads = [(slice(hh * GLA_DK, (hh + 1) * GLA_DK), slice(hh * GLA_DV, (hh + 1) * GLA_DV))
             for hh in range(GLA_HEADS)]

    b = jnp.concatenate(
        [_dot(tril2, jnp.concatenate([g_hi[rows], g_lo[rows]], axis=0)) for rows in chunk_rows],
        axis=0)
    b = b.reshape(nc, CHUNK, GLA_KD)
    b_ref = b[:, CHUNK // 2 - 1:CHUNK // 2, :]
    b_last = b[:, CHUNK - 1:CHUNK, :]
    v = _dot(h, win_ref[0, :, 2 * GLA_KD:2 * GLA_KD + GLA_VD]).astype(BF16)
    q = (qk[:, 0:GLA_KD] * (GLA_DK ** -0.5)).reshape(nc, CHUNK, GLA_KD)
    k = qk[:, GLA_KD:].reshape(nc, CHUNK, GLA_KD)
    qa = (q * jnp.exp2(b - b_ref)).astype(BF16)
    qi = (q * jnp.exp2(b)).astype(BF16)
    kb = (k * jnp.exp2(b_ref - b)).astype(BF16)
    kst = k * jnp.exp2(b_last - b)
    r = _dot(h, win_ref[0, :, 2 * GLA_KD + GLA_VD:])
    gate = r * jax.nn.sigmoid(r)

    kst_t = [kst[c].T.astype(BF16) for c in range(nc)]
    decay = [jnp.exp2(b[c, CHUNK - 8:CHUNK, :].T[:, 7:8]) for c in range(nc)]
    att = [[jnp.where(causal, _dot_nt(qa[c][:, kl], kb[c][:, kl]), 0.0).astype(BF16)
            for kl, _ in heads] for c in range(nc)]
    upd = [[_dot(kst_t[c][kl], v[chunk_rows[c], vl]) for kl, vl in heads] for c in range(nc)]

    o_rows = [[None] * GLA_HEADS for _ in range(nc)]
    for hh, (kl, vl) in enumerate(heads):
        st = state_ref[hh]
        for c in range(nc):
            lhs = jnp.concatenate([qi[c][:, kl], att[c][hh]], axis=1)
            rhs = jnp.concatenate([st.astype(BF16), v[chunk_rows[c], vl]], axis=0)
            o_rows[c][hh] = _dot(lhs, rhs)
            st = decay[c][kl] * st + upd[c][hh]
        state_ref[hh] = st
    o = jnp.concatenate([jnp.concatenate(o_rows[c], axis=1) for c in range(nc)], axis=0)

    gon = gon_ref[0]
    outs = [(_rms(o[:, vl], gon) * gate[:, vl]).astype(BF16) for _, vl in heads]
    o_ref[0] = x + _dot(jnp.concatenate(outs, axis=1), wout_ref[0])


def _gla(x, layer, gla_layer, gmix, w_in, w_g1, w_g2, b_g, g_on, w_out):
    b, s, d = x.shape
    tm = GLA_TM
    x_spec = pl.BlockSpec((1, tm, d), lambda bi, i: (bi, i, 0))
    return pl.pallas_call(
        functools.partial(_gla_kernel, tm=tm),
        grid=(b, s // tm),
        in_specs=[
            x_spec,
            _layer_block(gmix, layer),
            _layer_block(w_in, 0),
            _layer_block(w_g1, gla_layer),
            _layer_block(w_g2, gla_layer),
            _layer_block(b_g, gla_layer),
            _layer_block(g_on, gla_layer),
            _layer_block(w_out, 0),
        ],
        out_specs=x_spec,
        out_shape=jax.ShapeDtypeStruct((b, s, d), F32),
        scratch_shapes=[pltpu.VMEM((GLA_HEADS, GLA_DK, GLA_DV), F32)],
        compiler_params=_params(("arbitrary", "arbitrary")),
        name="gla",
    )(x, gmix, w_in, w_g1, w_g2, b_g, g_on, w_out)


def _rows(p):
    return p.reshape(p.shape[0], 1, p.shape[1])


def kernel(x, mem, n_ffn1, ffn1_w_gu, ffn1_w_down, n_mix, gla_w_in, gla_w_g1, gla_w_g2,
           gla_b_g, gla_onorm, gla_w_out, pool_w, pool_b, pool_scale, n_xattn, n_mem,
           x_w_q, x_w_kv, x_w_o, n_ffn2, ffn2_w_gu, ffn2_w_down, n_final):
    n_ffn1, n_mix, n_xattn, n_mem, n_ffn2, gla_b_g, gla_onorm, pool_scale = (
        _rows(p) for p in (n_ffn1, n_mix, n_xattn, n_mem, n_ffn2, gla_b_g, gla_onorm, pool_scale))
    gla_w_g1, gla_w_g2, pool_w = (w.astype(BF16) for w in (gla_w_g1, gla_w_g2, pool_w))
    kv, first_w = _mem_kv(mem, n_mem, x_w_kv,
                          casts=[(ffn1_w_gu, 0), (ffn1_w_down, 0), (gla_w_in, 0), (gla_w_out, 0)])
    ffn_w, gla_w = first_w[:2], first_w[2:]
    for i in range(DEPTH):
        x, (f2_gu, f2_down, wq, wo) = _ffn(
            x, i, n_ffn1, *ffn_w,
            casts=[(ffn2_w_gu, i), (ffn2_w_down, i), (x_w_q, i), (x_w_o, i)])
        j = i // 2
        if i % 2 == 0:
            x = _gla(x, i, j, n_mix, gla_w[0], gla_w_g1, gla_w_g2, gla_b_g, gla_onorm, gla_w[1])
            x = _xattn(x, i, kv, n_xattn, wq, wo)
        else:
            x = _pool_xattn(x, i, j, kv, n_mix, pool_w, pool_b, pool_scale, n_xattn, wq, wo)
        casts = []
        if i + 1 < DEPTH:
            casts = [(ffn1_w_gu, i + 1), (ffn1_w_down, i + 1)]
            if (i + 1) % 2 == 0:
                casts += [(gla_w_in, (i + 1) // 2), (gla_w_out, (i + 1) // 2)]
        g_final = n_final if i == DEPTH - 1 else None
        x, cast_outs = _ffn(x, i, n_ffn2, f2_gu, f2_down, g_final, casts=casts)
        ffn_w, gla_w = cast_outs[:2], cast_outs[2:]
    return x
```

```python
import functools

import jax
import jax.numpy as jnp
from jax import lax
from jax.experimental import pallas as pl
from jax.experimental.pallas import tpu as pltpu

F32 = jnp.float32
BF16 = jnp.bfloat16

D_MODEL = 1024
DEPTH = 4
EPS = 1e-6
D_FF = 2816
GLA_HEADS = 4
GLA_KD = 512
GLA_VD = 1024
GLA_DK = 128
GLA_DV = 256
GATE_TAU = 16.0
LOG2E = 1.4426950408889634
CHUNK = 64
POOL_WINDOWS = (2, 4, 8, 16)
POOL_GC = 256
POOL_HALO = 16
MEM_LEN = 256
X_HEADS = 4
X_DH = 256

MXU_TILE = 256
BF16_ROWS = 16
FFN_CHUNK = MXU_TILE
FFN_TM = 1024
KV_COLS = 2 * MXU_TILE
XATTN_TM = 1024
PREP_ROWS = 128
GLA_TM = 1024
VMEM_LIMIT = 56 * 1024 * 1024


def _rms(x, g):
    ms = jnp.mean(x * x, axis=-1, keepdims=True)
    return x * lax.rsqrt(ms + EPS) * g


def _layer_block(arr, layer):
    nd = arr.ndim
    return pl.BlockSpec((1,) + arr.shape[1:], lambda *_: (layer,) + (0,) * (nd - 1),
                        pipeline_mode=pl.Buffered(1))


def _dot(a, b):
    return jnp.dot(a, b, preferred_element_type=F32)


def _dot_nt(a, b):
    return lax.dot_general(a, b, (((1,), (1,)), ((), ())), preferred_element_type=F32)


def _params(semantics):
    return pltpu.CompilerParams(dimension_semantics=semantics, vmem_limit_bytes=VMEM_LIMIT)


def _cast_plan(w, layer, n_steps, step_of=lambda i: i):
    layers, rows, cols = w.shape
    rpb = next(k for k in range(BF16_ROWS, rows + 1, BF16_ROWS)
               if rows % k == 0 and rows // k <= n_steps)
    nblk = rows // rpb

    def blk(*idx):
        return jnp.minimum(step_of(*idx), nblk - 1)

    src = w.reshape(layers, nblk, rpb, cols)
    in_spec = pl.BlockSpec((1, 1, rpb, cols), lambda *idx: (layer, blk(*idx), 0, 0))
    out_spec = pl.BlockSpec((1, rpb, cols), lambda *idx: (blk(*idx), 0, 0))
    return src, in_spec, out_spec, jax.ShapeDtypeStruct((nblk, rpb, cols), BF16)


def _cast_jobs(casts, n_steps, step_of=lambda i: i):
    plans = [_cast_plan(w, layer, n_steps, step_of) for w, layer in casts]
    return tuple(list(col) for col in zip(*plans)) if plans else ([], [], [], [])


def _run_casts(src_refs, dst_refs):
    for src_ref, dst_ref in zip(src_refs, dst_refs):
        dst_ref[0] = src_ref[0, 0].astype(BF16)


def _cast_results(outs, casts):
    return [o.reshape(1, w.shape[1], w.shape[2]) for o, (w, _) in zip(outs, casts)]


def _ffn_chunks(x, g, wgu_ref, wd_ref, result):
    h = _rms(x, g).astype(BF16)
    acc = None
    for c in range(D_FF // FFN_CHUNK):
        cols = slice(c * FFN_CHUNK, (c + 1) * FFN_CHUNK)
        gate = _dot(h, wgu_ref[0, :, cols])
        up = _dot(h, wgu_ref[0, :, D_FF + c * FFN_CHUNK:D_FF + (c + 1) * FFN_CHUNK])
        a = (gate * jax.nn.sigmoid(gate) * up).astype(BF16)
        d = _dot(a, wd_ref[0, cols, :])
        acc = d if acc is None else acc + d
        yield
    result.append(x + 0.5 * acc)


def _ffn_kernel(*refs, final_norm, n_casts):
    x_ref, g_ref, wgu_ref, wd_ref = refs[:4]
    rest = refs[4:]
    if final_norm:
        gf_ref, rest = rest[0], rest[1:]
    cast_src, o_ref, cast_dst = rest[:n_casts], rest[n_casts], rest[n_casts + 1:]
    _run_casts(cast_src, cast_dst)
    result = []
    for _ in _ffn_chunks(x_ref[...], g_ref[0], wgu_ref, wd_ref, result):
        pass
    y = result[0]
    if final_norm:
        y = _rms(y, gf_ref[...])
    o_ref[...] = y


def _ffn(x, layer, g, w_gu, w_down, g_final=None, casts=()):
    b, s, d = x.shape
    t = b * s
    n_steps = t // FFN_TM
    final_norm = g_final is not None
    row_spec = pl.BlockSpec((FFN_TM, d), lambda i: (i, 0))
    in_specs = [row_spec, _layer_block(g, layer), _layer_block(w_gu, 0), _layer_block(w_down, 0)]
    args = [x.reshape(t, d), g, w_gu, w_down]
    if final_norm:
        in_specs.append(pl.BlockSpec((1, d), lambda i: (0, 0), pipeline_mode=pl.Buffered(1)))
        args.append(g_final.reshape(1, d))
    c_args, c_in, c_out, c_shapes = _cast_jobs(casts, n_steps)
    outs = pl.pallas_call(
        functools.partial(_ffn_kernel, final_norm=final_norm, n_casts=len(casts)),
        grid=(n_steps,),
        in_specs=in_specs + c_in,
        out_specs=[row_spec] + c_out,
        out_shape=[jax.ShapeDtypeStruct((t, d), F32)] + c_shapes,
        compiler_params=_params(("arbitrary",)),
        name="ffn",
    )(*args, *c_args)
    return outs[0].reshape(b, s, d), _cast_results(outs[1:], casts)


def _kv_kernel(mem_ref, g_ref, w_ref, *rest, n_casts):
    cast_src, o_ref, cast_dst, mn_ref = (rest[:n_casts], rest[n_casts],
                                         rest[n_casts + 1:-1], rest[-1])
    _run_casts(cast_src, cast_dst)

    @pl.when(pl.program_id(1) == 0)
    def _():
        mn_ref[...] = _rms(mem_ref[...], g_ref[0]).astype(BF16)

    o_ref[0] = _dot(mn_ref[...], w_ref[0].astype(BF16)).astype(BF16)


def _mem_kv(mem, n_mem, w_kv, casts=()):
    b, m, d = mem.shape
    depth = w_kv.shape[0]
    n_col = 2 * d // KV_COLS
    c_args, c_in, c_out, c_shapes = _cast_jobs(casts, depth * n_col, lambda l, j: l * n_col + j)
    outs = pl.pallas_call(
        functools.partial(_kv_kernel, n_casts=len(casts)),
        grid=(depth, n_col),
        in_specs=[
            pl.BlockSpec((b * m, d), lambda l, j: (0, 0), pipeline_mode=pl.Buffered(1)),
            pl.BlockSpec((1, 1, d), lambda l, j: (l, 0, 0)),
            pl.BlockSpec((1, d, KV_COLS), lambda l, j: (l, 0, j)),
        ] + c_in,
        out_specs=[pl.BlockSpec((1, b * m, KV_COLS), lambda l, j: (l, 0, j))] + c_out,
        out_shape=[jax.ShapeDtypeStruct((depth, b * m, 2 * d), BF16)] + c_shapes,
        scratch_shapes=[pltpu.VMEM((b * m, d), BF16)],
        compiler_params=_params(("arbitrary", "arbitrary")),
        name="mem_kv",
    )(mem.reshape(b * m, d), n_mem, w_kv, *c_args)
    return outs[0].reshape(depth, b, m, 2 * d), _cast_results(outs[1:], casts)


def _attend_stages(hn, x1, wq_ref, kv_ref, wo_ref, store):
    q = (_dot(hn, wq_ref[0]) * (X_DH ** -0.5)).astype(BF16)
    yield
    heads = [slice(h * X_DH, (h + 1) * X_DH) for h in range(X_HEADS)]
    scores = [_dot_nt(q[:, cols], kv_ref[0, 0, :, cols]) for cols in heads]
    yield
    probs = []
    for s in scores:
        e = jnp.exp(s - jnp.max(s, axis=-1, keepdims=True))
        probs.append((e * (1.0 / jnp.sum(e, axis=-1, keepdims=True))).astype(BF16))
    yield
    o = jnp.concatenate(
        [_dot(p, kv_ref[0, 0, :, D_MODEL + h * X_DH:D_MODEL + (h + 1) * X_DH]).astype(BF16)
         for h, p in enumerate(probs)], axis=1)
    yield
    store(x1 + _dot(o, wo_ref[0]))
    yield


def _xattn_kernel(x_ref, gx_ref, wq_ref, kv_ref, wo_ref, o_ref):
    x = x_ref[0]
    hn = _rms(x, gx_ref[0]).astype(BF16)

    def store(rows):
        o_ref[0] = rows

    for _ in _attend_stages(hn, x, wq_ref, kv_ref, wo_ref, store):
        pass


def _xattn(x, layer, kv, gx, wq, wo):
    b, s, d = x.shape
    tm = XATTN_TM
    x_spec = pl.BlockSpec((1, tm, d), lambda bi, i: (bi, i, 0))
    return pl.pallas_call(
        _xattn_kernel,
        grid=(b, s // tm),
        in_specs=[
            x_spec,
            _layer_block(gx, layer),
            _layer_block(wq, 0),
            pl.BlockSpec((1, 1, MEM_LEN, 2 * d), lambda bi, i: (layer, bi, 0, 0)),
            _layer_block(wo, 0),
        ],
        out_specs=x_spec,
        out_shape=jax.ShapeDtypeStruct((b, s, d), F32),
        compiler_params=_params(("arbitrary", "arbitrary")),
        name="xattn",
    )(x, gx, wq, kv, wo)


def _pool_windows(x, hprev, seq_start, gmix):
    rows = x.shape[0]
    hp = _rms(x, gmix)
    ext = jnp.concatenate([hprev, hp], axis=0)
    row = lax.broadcasted_iota(jnp.int32, (rows, 1), 0)
    ps = []
    for gi, w in enumerate(POOL_WINDOWS):
        lanes = slice(gi * POOL_GC, (gi + 1) * POOL_GC)
        win = ext[:, lanes]
        k = 1
        while k < w:
            win = win + pltpu.roll(win, k, axis=0)
            k *= 2
        if seq_start is False:
            inv_cnt = 1.0 / w
        else:
            inv_cnt = 1.0 / jnp.where(seq_start, jnp.minimum(row + 1, w), w).astype(F32)
        ps.append(win[POOL_HALO:] * inv_cnt - hp[:, lanes])
    return jnp.concatenate(ps, axis=1), hp[rows - POOL_HALO:]


def _interleave(main, side):
    for _ in main:
        next(side, None)
    for _ in side:
        pass


def _ffn_pool_kernel(x_ref, g_ref, wgu_ref, wd_ref, gmix_ref, pw_ref, pb_ref, ps_ref, *rest,
                     n_casts, n_tiles, tiles_per_seq):
    cast_src, o_ref, cast_dst = rest[:n_casts], rest[n_casts], rest[n_casts + 1:-2]
    y_ref, tail_ref = rest[-2:]
    t = pl.program_id(0)
    tm = x_ref.shape[0]
    _run_casts(cast_src, cast_dst)

    @pl.when(t == 0)
    def _():
        y_ref[...] = jnp.zeros_like(y_ref)
        tail_ref[...] = jnp.zeros_like(tail_ref)

    def pool_pieces():
        gmix = gmix_ref[0]
        seq_start = lax.rem(t + tiles_per_seq - 1, tiles_per_seq) == 0
        hprev = jnp.where(seq_start, 0.0, tail_ref[...])
        for r0 in range(0, tm, PREP_ROWS):
            y = y_ref[r0:r0 + PREP_ROWS, :]
            p, hprev = _pool_windows(y, hprev, seq_start if r0 == 0 else False, gmix)
            p = p.astype(BF16)
            mixed = jnp.concatenate(
                [_dot(p[:, gi * POOL_GC:(gi + 1) * POOL_GC], pw_ref[0, gi]) + pb_ref[0, gi:gi + 1, :]
                 for gi in range(len(POOL_WINDOWS))], axis=1)
            o_ref[r0:r0 + PREP_ROWS, :] = y + mixed * ps_ref[0]
            yield
        tail_ref[...] = hprev

    @pl.when(t < n_tiles)
    def _():
        result = []
        _interleave(_ffn_chunks(x_ref[...], g_ref[0], wgu_ref, wd_ref, result), pool_pieces())
        y_ref[...] = result[0]

    @pl.when(t == n_tiles)
    def _():
        for _ in pool_pieces():
            pass


def _ffn_pool(x, layer, g, w_gu, w_down, pool_layer, gmix, pw, pb, ps, casts=()):
    b, s, d = x.shape
    t = b * s
    tm = FFN_TM
    n_tiles = t // tm
    n_steps = n_tiles + 1
    c_args, c_in, c_out, c_shapes = _cast_jobs(casts, n_steps)
    outs = pl.pallas_call(
        functools.partial(_ffn_pool_kernel, n_casts=len(casts), n_tiles=n_tiles,
                          tiles_per_seq=s // tm),
        grid=(n_steps,),
        in_specs=[
            pl.BlockSpec((tm, d), lambda i: (jnp.minimum(i, n_tiles - 1), 0)),
            _layer_block(g, layer),
            _layer_block(w_gu, 0),
            _layer_block(w_down, 0),
            _layer_block(gmix, layer),
            _layer_block(pw, pool_layer),
            _layer_block(pb, pool_layer),
            _layer_block(ps, pool_layer),
        ] + c_in,
        out_specs=[pl.BlockSpec((tm, d), lambda i: (jnp.maximum(i - 1, 0), 0))] + c_out,
        out_shape=[jax.ShapeDtypeStruct((t, d), F32)] + c_shapes,
        scratch_shapes=[pltpu.VMEM((tm, d), F32), pltpu.VMEM((POOL_HALO, d), F32)],
        compiler_params=_params(("arbitrary",)),
        name="ffn_pool",
    )(x.reshape(t, d), g, w_gu, w_down, gmix, pw, pb, ps, *c_args)
    return outs[0].reshape(b, s, d), _cast_results(outs[1:], casts)


def _gla_kernel(x_ref, gmix_ref, win_ref, wg1_ref, wg2_ref, bg_ref, gon_ref, wout_ref,
                o_ref, state_ref, *, tm):
    @pl.when(pl.program_id(1) == 0)
    def _():
        state_ref[...] = jnp.zeros_like(state_ref)

    nc = tm // CHUNK
    x = x_ref[0]
    h = _rms(x, gmix_ref[0]).astype(BF16)
    gate_in = jnp.concatenate([_dot(h[:tm // 2], wg1_ref[0]), _dot(h[tm // 2:], wg1_ref[0])], axis=0)
    z = _dot(gate_in.astype(BF16), wg2_ref[0]) + bg_ref[0]
    qk = _dot(h, win_ref[0, :, 0:2 * GLA_KD])
    zs = z * LOG2E
    glog = (jnp.minimum(zs, 0.0) - jnp.log2(1.0 + jnp.exp2(-jnp.abs(zs)))) * (1.0 / GATE_TAU)
    g_hi = glog.astype(BF16)
    g_lo = (glog - g_hi.astype(F32)).astype(BF16)
    row = lax.broadcasted_iota(jnp.int32, (CHUNK, CHUNK), 0)
    col = lax.broadcasted_iota(jnp.int32, (CHUNK, CHUNK), 1)
    causal = row >= col
    tril = causal.astype(BF16)
    tril2 = jnp.concatenate([tril, tril], axis=1)
    chunk_rows = [slice(c * CHUNK, (c + 1) * CHUNK) for c in range(nc)]
    heads = [(slice(hh * GLA_DK, (hh + 1) * GLA_DK), slice(hh * GLA_DV, (hh + 1) * GLA_DV))
             for hh in range(GLA_HEADS)]

    b = jnp.concatenate(
        [_dot(tril2, jnp.concatenate([g_hi[rows], g_lo[rows]], axis=0)) for rows in chunk_rows],
        axis=0)
    b = b.reshape(nc, CHUNK, GLA_KD)
    b_ref = b[:, CHUNK // 2 - 1:CHUNK // 2, :]
    b_last = b[:, CHUNK - 1:CHUNK, :]
    v = _dot(h, win_ref[0, :, 2 * GLA_KD:2 * GLA_KD + GLA_VD]).astype(BF16)
    q = (qk[:, 0:GLA_KD] * (GLA_DK ** -0.5)).reshape(nc, CHUNK, GLA_KD)
    k = qk[:, GLA_KD:].reshape(nc, CHUNK, GLA_KD)
    qa = (q * jnp.exp2(b - b_ref)).astype(BF16)
    qi = (q * jnp.exp2(b)).astype(BF16)
    kb = (k * jnp.exp2(b_ref - b)).astype(BF16)
    kst = k * jnp.exp2(b_last - b)
    r = _dot(h, win_ref[0, :, 2 * GLA_KD + GLA_VD:])
    gate = r * jax.nn.sigmoid(r)

    kst_t = [kst[c].T.astype(BF16) for c in range(nc)]
    decay = [jnp.exp2(b[c, CHUNK - 8:CHUNK, :].T[:, 7:8]) for c in range(nc)]
    att = [[jnp.where(causal, _dot_nt(qa[c][:, kl], kb[c][:, kl]), 0.0).astype(BF16)
            for kl, _ in heads] for c in range(nc)]
    upd = [[_dot(kst_t[c][kl], v[chunk_rows[c], vl]) for kl, vl in heads] for c in range(nc)]

    o_rows = [[None] * GLA_HEADS for _ in range(nc)]
    for hh, (kl, vl) in enumerate(heads):
        st = state_ref[hh]
        for c in range(nc):
            lhs = jnp.concatenate([qi[c][:, kl], att[c][hh]], axis=1)
            rhs = jnp.concatenate([st.astype(BF16), v[chunk_rows[c], vl]], axis=0)
            o_rows[c][hh] = _dot(lhs, rhs)
            st = decay[c][kl] * st + upd[c][hh]
        state_ref[hh] = st
    o = jnp.concatenate([jnp.concatenate(o_rows[c], axis=1) for c in range(nc)], axis=0)

    gon = gon_ref[0]
    outs = [(_rms(o[:, vl], gon) * gate[:, vl]).astype(BF16) for _, vl in heads]
    o_ref[0] = x + _dot(jnp.concatenate(outs, axis=1), wout_ref[0])


def _gla(x, layer, gla_layer, gmix, w_in, w_g1, w_g2, b_g, g_on, w_out):
    b, s, d = x.shape
    tm = GLA_TM
    x_spec = pl.BlockSpec((1, tm, d), lambda bi, i: (bi, i, 0))
    return pl.pallas_call(
        functools.partial(_gla_kernel, tm=tm),
        grid=(b, s // tm),
        in_specs=[
            x_spec,
            _layer_block(gmix, layer),
            _layer_block(w_in, 0),
            _layer_block(w_g1, gla_layer),
            _layer_block(w_g2, gla_layer),
            _layer_block(b_g, gla_layer),
            _layer_block(g_on, gla_layer),
            _layer_block(w_out, 0),
        ],
        out_specs=x_spec,
        out_shape=jax.ShapeDtypeStruct((b, s, d), F32),
        scratch_shapes=[pltpu.VMEM((GLA_HEADS, GLA_DK, GLA_DV), F32)],
        compiler_params=_params(("arbitrary", "arbitrary")),
        name="gla",
    )(x, gmix, w_in, w_g1, w_g2, b_g, g_on, w_out)


def _rows(p):
    return p.reshape(p.shape[0], 1, p.shape[1])


def kernel(x, mem, n_ffn1, ffn1_w_gu, ffn1_w_down, n_mix, gla_w_in, gla_w_g1, gla_w_g2,
           gla_b_g, gla_onorm, gla_w_out, pool_w, pool_b, pool_scale, n_xattn, n_mem,
           x_w_q, x_w_kv, x_w_o, n_ffn2, ffn2_w_gu, ffn2_w_down, n_final):
    n_ffn1, n_mix, n_xattn, n_mem, n_ffn2, gla_b_g, gla_onorm, pool_scale = (
        _rows(p) for p in (n_ffn1, n_mix, n_xattn, n_mem, n_ffn2, gla_b_g, gla_onorm, pool_scale))
    gla_w_g1, gla_w_g2, pool_w = (w.astype(BF16) for w in (gla_w_g1, gla_w_g2, pool_w))
    kv, first_w = _mem_kv(mem, n_mem, x_w_kv,
                          casts=[(ffn1_w_gu, 0), (ffn1_w_down, 0), (gla_w_in, 0), (gla_w_out, 0)])
    ffn_w, gla_w = first_w[:2], first_w[2:]
    for i in range(DEPTH):
        casts = [(ffn2_w_gu, i), (ffn2_w_down, i), (x_w_q, i), (x_w_o, i)]
        j = i // 2
        if i % 2 == 0:
            x, (f2_gu, f2_down, wq, wo) = _ffn(x, i, n_ffn1, *ffn_w, casts=casts)
            x = _gla(x, i, j, n_mix, gla_w[0], gla_w_g1, gla_w_g2, gla_b_g, gla_onorm, gla_w[1])
        else:
            x, (f2_gu, f2_down, wq, wo) = _ffn_pool(
                x, i, n_ffn1, *ffn_w, j, n_mix, pool_w, pool_b, pool_scale, casts=casts)
        x = _xattn(x, i, kv, n_xattn, wq, wo)
        casts = []
        if i + 1 < DEPTH:
            casts = [(ffn1_w_gu, i + 1), (ffn1_w_down, i + 1)]
            if (i + 1) % 2 == 0:
                casts += [(gla_w_in, (i + 1) // 2), (gla_w_out, (i + 1) // 2)]
        g_final = n_final if i == DEPTH - 1 else None
        x, cast_outs = _ffn(x, i, n_ffn2, f2_gu, f2_down, g_final, casts=casts)
        ffn_w, gla_w = cast_outs[:2], cast_outs[2:]
    return x
```

```python
import functools

import jax
import jax.numpy as jnp
from jax import lax
from jax.experimental import pallas as pl
from jax.experimental.pallas import tpu as pltpu

F32 = jnp.float32
BF16 = jnp.bfloat16

D_MODEL = 1024
DEPTH = 4
EPS = 1e-6
D_FF = 2816
GLA_HEADS = 4
GLA_KD = 512
GLA_VD = 1024
GLA_DK = 128
GLA_DV = 256
GATE_TAU = 16.0
LOG2E = 1.4426950408889634
CHUNK = 64
POOL_WINDOWS = (2, 4, 8, 16)
POOL_GC = 256
POOL_HALO = 16
MEM_LEN = 256
X_HEADS = 4
X_DH = 256

MXU_TILE = 256
BF16_ROWS = 16
FFN_CHUNK = MXU_TILE
FFN_TM = 1024
KV_COLS = 2 * MXU_TILE
XATTN_TM = 1024
PREP_ROWS = 128
GLA_TM = 1024
VMEM_LIMIT = 56 * 1024 * 1024


def _rms(x, g):
    ms = jnp.mean(x * x, axis=-1, keepdims=True)
    return x * lax.rsqrt(ms + EPS) * g


def _layer_block(arr, layer):
    nd = arr.ndim
    return pl.BlockSpec((1,) + arr.shape[1:], lambda *_: (layer,) + (0,) * (nd - 1),
                        pipeline_mode=pl.Buffered(1))


def _dot(a, b):
    return jnp.dot(a, b, preferred_element_type=F32)


def _dot_nt(a, b):
    return lax.dot_general(a, b, (((1,), (1,)), ((), ())), preferred_element_type=F32)


def _params(semantics):
    return pltpu.CompilerParams(dimension_semantics=semantics, vmem_limit_bytes=VMEM_LIMIT)


def _cast_plan(w, layer, n_steps, step_of=lambda i: i):
    layers, rows, cols = w.shape
    rpb = next(k for k in range(BF16_ROWS, rows + 1, BF16_ROWS)
               if rows % k == 0 and rows // k <= n_steps)
    nblk = rows // rpb

    def blk(*idx):
        return jnp.minimum(step_of(*idx), nblk - 1)

    src = w.reshape(layers, nblk, rpb, cols)
    in_spec = pl.BlockSpec((1, 1, rpb, cols), lambda *idx: (layer, blk(*idx), 0, 0))
    out_spec = pl.BlockSpec((1, rpb, cols), lambda *idx: (blk(*idx), 0, 0))
    return src, in_spec, out_spec, jax.ShapeDtypeStruct((nblk, rpb, cols), BF16)


def _cast_jobs(casts, n_steps, step_of=lambda i: i):
    plans = [_cast_plan(w, layer, n_steps, step_of) for w, layer in casts]
    return tuple(list(col) for col in zip(*plans)) if plans else ([], [], [], [])


def _run_casts(src_refs, dst_refs):
    for src_ref, dst_ref in zip(src_refs, dst_refs):
        dst_ref[0] = src_ref[0, 0].astype(BF16)


def _cast_results(outs, casts):
    return [o.reshape(1, w.shape[1], w.shape[2]) for o, (w, _) in zip(outs, casts)]


def _ffn_chunks(x, g, wgu_ref, wd_ref, result):
    h = _rms(x, g).astype(BF16)
    acc = None
    for c in range(D_FF // FFN_CHUNK):
        cols = slice(c * FFN_CHUNK, (c + 1) * FFN_CHUNK)
        gate = _dot(h, wgu_ref[0, :, cols])
        up = _dot(h, wgu_ref[0, :, D_FF + c * FFN_CHUNK:D_FF + (c + 1) * FFN_CHUNK])
        a = (gate * jax.nn.sigmoid(gate) * up).astype(BF16)
        d = _dot(a, wd_ref[0, cols, :])
        acc = d if acc is None else acc + d
        yield
    result.append(x + 0.5 * acc)


def _ffn_kernel(*refs, final_norm, n_casts):
    x_ref, g_ref, wgu_ref, wd_ref = refs[:4]
    rest = refs[4:]
    if final_norm:
        gf_ref, rest = rest[0], rest[1:]
    cast_src, o_ref, cast_dst = rest[:n_casts], rest[n_casts], rest[n_casts + 1:]
    _run_casts(cast_src, cast_dst)
    result = []
    for _ in _ffn_chunks(x_ref[...], g_ref[0], wgu_ref, wd_ref, result):
        pass
    y = result[0]
    if final_norm:
        y = _rms(y, gf_ref[...])
    o_ref[...] = y


def _ffn(x, layer, g, w_gu, w_down, g_final=None, casts=()):
    b, s, d = x.shape
    t = b * s
    n_steps = t // FFN_TM
    final_norm = g_final is not None
    row_spec = pl.BlockSpec((FFN_TM, d), lambda i: (i, 0))
    in_specs = [row_spec, _layer_block(g, layer), _layer_block(w_gu, 0), _layer_block(w_down, 0)]
    args = [x.reshape(t, d), g, w_gu, w_down]
    if final_norm:
        in_specs.append(pl.BlockSpec((1, d), lambda i: (0, 0), pipeline_mode=pl.Buffered(1)))
        args.append(g_final.reshape(1, d))
    c_args, c_in, c_out, c_shapes = _cast_jobs(casts, n_steps)
    outs = pl.pallas_call(
        functools.partial(_ffn_kernel, final_norm=final_norm, n_casts=len(casts)),
        grid=(n_steps,),
        in_specs=in_specs + c_in,
        out_specs=[row_spec] + c_out,
        out_shape=[jax.ShapeDtypeStruct((t, d), F32)] + c_shapes,
        compiler_params=_params(("arbitrary",)),
        name="ffn",
    )(*args, *c_args)
    return outs[0].reshape(b, s, d), _cast_results(outs[1:], casts)


def _kv_kernel(mem_ref, g_ref, w_ref, *rest, n_casts):
    cast_src, o_ref, cast_dst, mn_ref = (rest[:n_casts], rest[n_casts],
                                         rest[n_casts + 1:-1], rest[-1])
    _run_casts(cast_src, cast_dst)

    @pl.when(pl.program_id(1) == 0)
    def _():
        mn_ref[...] = _rms(mem_ref[...], g_ref[0]).astype(BF16)

    o_ref[0] = _dot(mn_ref[...], w_ref[0].astype(BF16)).astype(BF16)


def _mem_kv(mem, n_mem, w_kv, casts=()):
    b, m, d = mem.shape
    depth = w_kv.shape[0]
    n_col = 2 * d // KV_COLS
    c_args, c_in, c_out, c_shapes = _cast_jobs(casts, depth * n_col, lambda l, j: l * n_col + j)
    outs = pl.pallas_call(
        functools.partial(_kv_kernel, n_casts=len(casts)),
        grid=(depth, n_col),
        in_specs=[
            pl.BlockSpec((b * m, d), lambda l, j: (0, 0), pipeline_mode=pl.Buffered(1)),
            pl.BlockSpec((1, 1, d), lambda l, j: (l, 0, 0)),
            pl.BlockSpec((1, d, KV_COLS), lambda l, j: (l, 0, j)),
        ] + c_in,
        out_specs=[pl.BlockSpec((1, b * m, KV_COLS), lambda l, j: (l, 0, j))] + c_out,
        out_shape=[jax.ShapeDtypeStruct((depth, b * m, 2 * d), BF16)] + c_shapes,
        scratch_shapes=[pltpu.VMEM((b * m, d), BF16)],
        compiler_params=_params(("arbitrary", "arbitrary")),
        name="mem_kv",
    )(mem.reshape(b * m, d), n_mem, w_kv, *c_args)
    return outs[0].reshape(depth, b, m, 2 * d), _cast_results(outs[1:], casts)


def _attend_stages(hn, x1, wq_ref, kv_ref, wo_ref, store):
    q = (_dot(hn, wq_ref[0]) * (X_DH ** -0.5)).astype(BF16)
    yield
    heads = [slice(h * X_DH, (h + 1) * X_DH) for h in range(X_HEADS)]
    scores = [_dot_nt(q[:, cols], kv_ref[0, 0, :, cols]) for cols in heads]
    yield
    probs = []
    for s in scores:
        e = jnp.exp(s - jnp.max(s, axis=-1, keepdims=True))
        probs.append((e * (1.0 / jnp.sum(e, axis=-1, keepdims=True))).astype(BF16))
    yield
    o = jnp.concatenate(
        [_dot(p, kv_ref[0, 0, :, D_MODEL + h * X_DH:D_MODEL + (h + 1) * X_DH]).astype(BF16)
         for h, p in enumerate(probs)], axis=1)
    yield
    store(x1 + _dot(o, wo_ref[0]))
    yield


def _xattn_kernel(x_ref, gx_ref, wq_ref, kv_ref, wo_ref, o_ref):
    x = x_ref[0]
    hn = _rms(x, gx_ref[0]).astype(BF16)

    def store(rows):
        o_ref[0] = rows

    for _ in _attend_stages(hn, x, wq_ref, kv_ref, wo_ref, store):
        pass


def _xattn(x, layer, kv, gx, wq, wo):
    b, s, d = x.shape
    tm = XATTN_TM
    x_spec = pl.BlockSpec((1, tm, d), lambda bi, i: (bi, i, 0))
    return pl.pallas_call(
        _xattn_kernel,
        grid=(b, s // tm),
        in_specs=[
            x_spec,
            _layer_block(gx, layer),
            _layer_block(wq, 0),
            pl.BlockSpec((1, 1, MEM_LEN, 2 * d), lambda bi, i: (layer, bi, 0, 0)),
            _layer_block(wo, 0),
        ],
        out_specs=x_spec,
        out_shape=jax.ShapeDtypeStruct((b, s, d), F32),
        compiler_params=_params(("arbitrary", "arbitrary")),
        name="xattn",
    )(x, gx, wq, kv, wo)


def _pool_windows(x, hprev, seq_start, gmix):
    rows = x.shape[0]
    hp = _rms(x, gmix)
    ext = jnp.concatenate([hprev, hp], axis=0)
    row = lax.broadcasted_iota(jnp.int32, (rows, 1), 0)
    ps = []
    for gi, w in enumerate(POOL_WINDOWS):
        lanes = slice(gi * POOL_GC, (gi + 1) * POOL_GC)
        win = ext[:, lanes]
        k = 1
        while k < w:
            win = win + pltpu.roll(win, k, axis=0)
            k *= 2
        if seq_start is False:
            inv_cnt = 1.0 / w
        else:
            inv_cnt = 1.0 / jnp.where(seq_start, jnp.minimum(row + 1, w), w).astype(F32)
        ps.append(win[POOL_HALO:] * inv_cnt - hp[:, lanes])
    return jnp.concatenate(ps, axis=1), hp[rows - POOL_HALO:]


def _interleave(main, side):
    for _ in main:
        next(side, None)
    for _ in side:
        pass


def _ffn_pool_kernel(x_ref, g_ref, wgu_ref, wd_ref, gmix_ref, pw_ref, pb_ref, ps_ref, *rest,
                     n_casts, n_tiles, tiles_per_seq):
    cast_src, o_ref, cast_dst = rest[:n_casts], rest[n_casts], rest[n_casts + 1:-2]
    y_ref, tail_ref = rest[-2:]
    t = pl.program_id(0)
    tm = x_ref.shape[0]
    _run_casts(cast_src, cast_dst)

    @pl.when(t == 0)
    def _():
        y_ref[...] = jnp.zeros_like(y_ref)
        tail_ref[...] = jnp.zeros_like(tail_ref)

    def pool_pieces():
        gmix = gmix_ref[0]
        seq_start = lax.rem(t + tiles_per_seq - 1, tiles_per_seq) == 0
        hprev = jnp.where(seq_start, 0.0, tail_ref[...])
        for r0 in range(0, tm, PREP_ROWS):
            y = y_ref[r0:r0 + PREP_ROWS, :]
            p, hprev = _pool_windows(y, hprev, seq_start if r0 == 0 else False, gmix)
            p = p.astype(BF16)
            mixed = jnp.concatenate(
                [_dot(p[:, gi * POOL_GC:(gi + 1) * POOL_GC], pw_ref[0, gi]) + pb_ref[0, gi:gi + 1, :]
                 for gi in range(len(POOL_WINDOWS))], axis=1)
            o_ref[r0:r0 + PREP_ROWS, :] = y + mixed * ps_ref[0]
            yield
        tail_ref[...] = hprev

    @pl.when(t < n_tiles)
    def _():
        result = []
        _interleave(_ffn_chunks(x_ref[...], g_ref[0], wgu_ref, wd_ref, result), pool_pieces())
        y_ref[...] = result[0]

    @pl.when(t == n_tiles)
    def _():
        for _ in pool_pieces():
            pass


def _ffn_pool(x, layer, g, w_gu, w_down, pool_layer, gmix, pw, pb, ps, casts=()):
    b, s, d = x.shape
    t = b * s
    tm = FFN_TM
    n_tiles = t // tm
    n_steps = n_tiles + 1
    c_args, c_in, c_out, c_shapes = _cast_jobs(casts, n_steps)
    outs = pl.pallas_call(
        functools.partial(_ffn_pool_kernel, n_casts=len(casts), n_tiles=n_tiles,
                          tiles_per_seq=s // tm),
        grid=(n_steps,),
        in_specs=[
            pl.BlockSpec((tm, d), lambda i: (jnp.minimum(i, n_tiles - 1), 0)),
            _layer_block(g, layer),
            _layer_block(w_gu, 0),
            _layer_block(w_down, 0),
            _layer_block(gmix, layer),
            _layer_block(pw, pool_layer),
            _layer_block(pb, pool_layer),
            _layer_block(ps, pool_layer),
        ] + c_in,
        out_specs=[pl.BlockSpec((tm, d), lambda i: (jnp.maximum(i - 1, 0), 0))] + c_out,
        out_shape=[jax.ShapeDtypeStruct((t, d), F32)] + c_shapes,
        scratch_shapes=[pltpu.VMEM((tm, d), F32), pltpu.VMEM((POOL_HALO, d), F32)],
        compiler_params=_params(("arbitrary",)),
        name="ffn_pool",
    )(x.reshape(t, d), g, w_gu, w_down, gmix, pw, pb, ps, *c_args)
    return outs[0].reshape(b, s, d), _cast_results(outs[1:], casts)


def _gla_kernel(x_ref, gmix_ref, win_ref, wg1_ref, wg2_ref, bg_ref, gon_ref, wout_ref,
                o_ref, state_ref, *, tm):
    @pl.when(pl.program_id(1) == 0)
    def _():
        state_ref[...] = jnp.zeros_like(state_ref)

    nc = tm // CHUNK
    x = x_ref[0]
    h = _rms(x, gmix_ref[0]).astype(BF16)
    gate_in = jnp.concatenate([_dot(h[:tm // 2], wg1_ref[0]), _dot(h[tm // 2:], wg1_ref[0])], axis=0)
    z = _dot(gate_in.astype(BF16), wg2_ref[0]) + bg_ref[0]
    qk = _dot(h, win_ref[0, :, 0:2 * GLA_KD])
    zs = z * LOG2E
    glog = (jnp.minimum(zs, 0.0) - jnp.log2(1.0 + jnp.exp2(-jnp.abs(zs)))) * (1.0 / GATE_TAU)
    g_hi = glog.astype(BF16)
    g_lo = (glog - g_hi.astype(F32)).astype(BF16)
    row = lax.broadcasted_iota(jnp.int32, (CHUNK, CHUNK), 0)
    col = lax.broadcasted_iota(jnp.int32, (CHUNK, CHUNK), 1)
    causal = row >= col
    tril = causal.astype(BF16)
    tril2 = jnp.concatenate([tril, tril], axis=1)
    chunk_rows = [slice(c * CHUNK, (c + 1) * CHUNK) for c in range(nc)]
    heads = [(slice(hh * GLA_DK, (hh + 1) * GLA_DK), slice(hh * GLA_DV, (hh + 1) * GLA_DV))
             for hh in range(GLA_HEADS)]

    b = jnp.concatenate(
        [_dot(tril2, jnp.concatenate([g_hi[rows], g_lo[rows]], axis=0)) for rows in chunk_rows],
        axis=0)
    b = b.reshape(nc, CHUNK, GLA_KD)
    b_ref = b[:, CHUNK // 2 - 1:CHUNK // 2, :]
    b_last = b[:, CHUNK - 1:CHUNK, :]
    v = _dot(h, win_ref[0, :, 2 * GLA_KD:2 * GLA_KD + GLA_VD]).astype(BF16)
    q = (qk[:, 0:GLA_KD] * (GLA_DK ** -0.5)).reshape(nc, CHUNK, GLA_KD)
    k = qk[:, GLA_KD:].reshape(nc, CHUNK, GLA_KD)
    qa = (q * jnp.exp2(b - b_ref)).astype(BF16)
    qi = (q * jnp.exp2(b)).astype(BF16)
    kb = (k * jnp.exp2(b_ref - b)).astype(BF16)
    kst = k * jnp.exp2(b_last - b)
    r = _dot(h, win_ref[0, :, 2 * GLA_KD + GLA_VD:])
    gate = r * jax.nn.sigmoid(r)

    n_pairs = nc // 2
    pair_rows = [slice(p * 2 * CHUNK, (p + 1) * 2 * CHUNK) for p in range(n_pairs)]

    def even(t):
        return t.reshape((n_pairs, 2) + t.shape[1:])[:, 0]

    def odd(t):
        return t.reshape((n_pairs, 2) + t.shape[1:])[:, 1]

    last_a, last_b = even(b_last), odd(b_last)
    qi_a = even(qi)
    qi_b = odd(qi)
    qid_b = (odd(q) * jnp.exp2(odd(b) + last_a)).astype(BF16)
    kst_a = even(kst).astype(BF16)
    kstd_a = even(k) * jnp.exp2(last_a - even(b) + last_b)
    kst_b = odd(kst)
    kt = [jnp.concatenate([kstd_a[p].T, kst_b[p].T], axis=1).astype(BF16)
          for p in range(n_pairs)]
    tail = even(b)[:, CHUNK - 8:CHUNK, :] + odd(b)[:, CHUNK - 8:CHUNK, :]
    decay = [jnp.exp2(tail[p].T[:, 7:8]) for p in range(n_pairs)]

    def masked(c, kl):
        return jnp.where(causal, _dot_nt(qa[c][:, kl], kb[c][:, kl]), 0.0).astype(BF16)

    zero_blk = jnp.zeros((CHUNK, CHUNK), BF16)
    lhs = [[jnp.concatenate(
        [jnp.concatenate([qi_a[p][:, kl], masked(2 * p, kl), zero_blk], axis=1),
         jnp.concatenate([qid_b[p][:, kl], _dot_nt(qi_b[p][:, kl], kst_a[p][:, kl]).astype(BF16),
                          masked(2 * p + 1, kl)], axis=1)], axis=0)
        for kl, _ in heads] for p in range(n_pairs)]
    upd = [[_dot(kt[p][kl], v[pair_rows[p], vl]) for kl, vl in heads] for p in range(n_pairs)]

    o_rows = [[None] * GLA_HEADS for _ in range(n_pairs)]
    for hh, (kl, vl) in enumerate(heads):
        st = state_ref[hh]
        for p in range(n_pairs):
            rhs = jnp.concatenate([st.astype(BF16), v[pair_rows[p], vl]], axis=0)
            o_rows[p][hh] = _dot(lhs[p][hh], rhs)
            st = decay[p][kl] * st + upd[p][hh]
        state_ref[hh] = st
    o = jnp.concatenate([jnp.concatenate(o_rows[p], axis=1) for p in range(n_pairs)], axis=0)

    gon = gon_ref[0]
    outs = [(_rms(o[:, vl], gon) * gate[:, vl]).astype(BF16) for _, vl in heads]
    o_ref[0] = x + _dot(jnp.concatenate(outs, axis=1), wout_ref[0])


def _gla(x, layer, gla_layer, gmix, w_in, w_g1, w_g2, b_g, g_on, w_out):
    b, s, d = x.shape
    tm = GLA_TM
    x_spec = pl.BlockSpec((1, tm, d), lambda bi, i: (bi, i, 0))
    return pl.pallas_call(
        functools.partial(_gla_kernel, tm=tm),
        grid=(b, s // tm),
        in_specs=[
            x_spec,
            _layer_block(gmix, layer),
            _layer_block(w_in, 0),
            _layer_block(w_g1, gla_layer),
            _layer_block(w_g2, gla_layer),
            _layer_block(b_g, gla_layer),
            _layer_block(g_on, gla_layer),
            _layer_block(w_out, 0),
        ],
        out_specs=x_spec,
        out_shape=jax.ShapeDtypeStruct((b, s, d), F32),
        scratch_shapes=[pltpu.VMEM((GLA_HEADS, GLA_DK, GLA_DV), F32)],
        compiler_params=_params(("arbitrary", "arbitrary")),
        name="gla",
    )(x, gmix, w_in, w_g1, w_g2, b_g, g_on, w_out)


def _rows(p):
    return p.reshape(p.shape[0], 1, p.shape[1])


def kernel(x, mem, n_ffn1, ffn1_w_gu, ffn1_w_down, n_mix, gla_w_in, gla_w_g1, gla_w_g2,
           gla_b_g, gla_onorm, gla_w_out, pool_w, pool_b, pool_scale, n_xattn, n_mem,
           x_w_q, x_w_kv, x_w_o, n_ffn2, ffn2_w_gu, ffn2_w_down, n_final):
    n_ffn1, n_mix, n_xattn, n_mem, n_ffn2, gla_b_g, gla_onorm, pool_scale = (
        _rows(p) for p in (n_ffn1, n_mix, n_xattn, n_mem, n_ffn2, gla_b_g, gla_onorm, pool_scale))
    gla_w_g1, gla_w_g2, pool_w = (w.astype(BF16) for w in (gla_w_g1, gla_w_g2, pool_w))
    kv, first_w = _mem_kv(mem, n_mem, x_w_kv,
                          casts=[(ffn1_w_gu, 0), (ffn1_w_down, 0), (gla_w_in, 0), (gla_w_out, 0)])
    ffn_w, gla_w = first_w[:2], first_w[2:]
    for i in range(DEPTH):
        casts = [(ffn2_w_gu, i), (ffn2_w_down, i), (x_w_q, i), (x_w_o, i)]
        j = i // 2
        if i % 2 == 0:
            x, (f2_gu, f2_down, wq, wo) = _ffn(x, i, n_ffn1, *ffn_w, casts=casts)
            x = _gla(x, i, j, n_mix, gla_w[0], gla_w_g1, gla_w_g2, gla_b_g, gla_onorm, gla_w[1])
        else:
            x, (f2_gu, f2_down, wq, wo) = _ffn_pool(
                x, i, n_ffn1, *ffn_w, j, n_mix, pool_w, pool_b, pool_scale, casts=casts)
        x = _xattn(x, i, kv, n_xattn, wq, wo)
        casts = []
        if i + 1 < DEPTH:
            casts = [(ffn1_w_gu, i + 1), (ffn1_w_down, i + 1)]
            if (i + 1) % 2 == 0:
                casts += [(gla_w_in, (i + 1) // 2), (gla_w_out, (i + 1) // 2)]
        g_final = n_final if i == DEPTH - 1 else None
        x, cast_outs = _ffn(x, i, n_ffn2, f2_gu, f2_down, g_final, casts=casts)
        ffn_w, gla_w = cast_outs[:2], cast_outs[2:]
    return x
```

```python
import functools

import jax
import jax.numpy as jnp
from jax import lax
from jax.experimental import pallas as pl
from jax.experimental.pallas import tpu as pltpu

F32 = jnp.float32
BF16 = jnp.bfloat16

D_MODEL = 1024
DEPTH = 4
EPS = 1e-6
D_FF = 2816
GLA_HEADS = 4
GLA_KD = 512
GLA_VD = 1024
GLA_DK = 128
GLA_DV = 256
GATE_TAU = 16.0
LOG2E = 1.4426950408889634
CHUNK = 64
POOL_WINDOWS = (2, 4, 8, 16)
POOL_GC = 256
POOL_HALO = 16
MEM_LEN = 256
X_HEADS = 4
X_DH = 256

MXU_TILE = 256
BF16_ROWS = 16
FFN_CHUNK = MXU_TILE
FFN_TM = 1024
KV_COLS = 2 * MXU_TILE
XATTN_TM = 1024
PREP_ROWS = 128
GLA_TM = 1024
VMEM_LIMIT = 56 * 1024 * 1024


def _rms(x, g):
    ms = jnp.mean(x * x, axis=-1, keepdims=True)
    return x * lax.rsqrt(ms + EPS) * g


def _layer_block(arr, layer):
    nd = arr.ndim
    return pl.BlockSpec((1,) + arr.shape[1:], lambda *_: (layer,) + (0,) * (nd - 1),
                        pipeline_mode=pl.Buffered(1))


def _dot(a, b):
    return jnp.dot(a, b, preferred_element_type=F32)


def _dot_nt(a, b):
    return lax.dot_general(a, b, (((1,), (1,)), ((), ())), preferred_element_type=F32)


def _params(semantics):
    return pltpu.CompilerParams(dimension_semantics=semantics, vmem_limit_bytes=VMEM_LIMIT)


def _cast_plan(w, layer, n_steps, step_of=lambda i: i):
    layers, rows, cols = w.shape
    rpb = next(k for k in range(BF16_ROWS, rows + 1, BF16_ROWS)
               if rows % k == 0 and rows // k <= n_steps)
    nblk = rows // rpb

    def blk(*idx):
        return jnp.minimum(step_of(*idx), nblk - 1)

    src = w.reshape(layers, nblk, rpb, cols)
    in_spec = pl.BlockSpec((1, 1, rpb, cols), lambda *idx: (layer, blk(*idx), 0, 0))
    out_spec = pl.BlockSpec((1, rpb, cols), lambda *idx: (blk(*idx), 0, 0))
    return src, in_spec, out_spec, jax.ShapeDtypeStruct((nblk, rpb, cols), BF16)


def _cast_jobs(casts, n_steps, step_of=lambda i: i):
    plans = [_cast_plan(w, layer, n_steps, step_of) for w, layer in casts]
    return tuple(list(col) for col in zip(*plans)) if plans else ([], [], [], [])


def _run_casts(src_refs, dst_refs):
    for src_ref, dst_ref in zip(src_refs, dst_refs):
        dst_ref[0] = src_ref[0, 0].astype(BF16)


def _cast_results(outs, casts):
    return [o.reshape(1, w.shape[1], w.shape[2]) for o, (w, _) in zip(outs, casts)]


def _ffn_chunks(x, g, wgu_ref, wd_ref, result):
    h = _rms(x, g).astype(BF16)
    acc = None
    for c in range(D_FF // FFN_CHUNK):
        cols = slice(c * FFN_CHUNK, (c + 1) * FFN_CHUNK)
        gate = _dot(h, wgu_ref[0, :, cols])
        up = _dot(h, wgu_ref[0, :, D_FF + c * FFN_CHUNK:D_FF + (c + 1) * FFN_CHUNK])
        a = (gate * jax.nn.sigmoid(gate) * up).astype(BF16)
        d = _dot(a, wd_ref[0, cols, :])
        acc = d if acc is None else acc + d
        yield
    result.append(x + 0.5 * acc)


def _ffn_kernel(*refs, final_norm, n_casts):
    x_ref, g_ref, wgu_ref, wd_ref = refs[:4]
    rest = refs[4:]
    if final_norm:
        gf_ref, rest = rest[0], rest[1:]
    cast_src, o_ref, cast_dst = rest[:n_casts], rest[n_casts], rest[n_casts + 1:]
    _run_casts(cast_src, cast_dst)
    result = []
    for _ in _ffn_chunks(x_ref[...], g_ref[0], wgu_ref, wd_ref, result):
        pass
    y = result[0]
    if final_norm:
        y = _rms(y, gf_ref[...])
    o_ref[...] = y


def _ffn(x, layer, g, w_gu, w_down, g_final=None, casts=()):
    b, s, d = x.shape
    t = b * s
    n_steps = t // FFN_TM
    final_norm = g_final is not None
    row_spec = pl.BlockSpec((FFN_TM, d), lambda i: (i, 0))
    in_specs = [row_spec, _layer_block(g, layer), _layer_block(w_gu, 0), _layer_block(w_down, 0)]
    args = [x.reshape(t, d), g, w_gu, w_down]
    if final_norm:
        in_specs.append(pl.BlockSpec((1, d), lambda i: (0, 0), pipeline_mode=pl.Buffered(1)))
        args.append(g_final.reshape(1, d))
    c_args, c_in, c_out, c_shapes = _cast_jobs(casts, n_steps)
    outs = pl.pallas_call(
        functools.partial(_ffn_kernel, final_norm=final_norm, n_casts=len(casts)),
        grid=(n_steps,),
        in_specs=in_specs + c_in,
        out_specs=[row_spec] + c_out,
        out_shape=[jax.ShapeDtypeStruct((t, d), F32)] + c_shapes,
        compiler_params=_params(("arbitrary",)),
        name="ffn",
    )(*args, *c_args)
    return outs[0].reshape(b, s, d), _cast_results(outs[1:], casts)


def _kv_kernel(mem_ref, g_ref, w_ref, *rest, n_casts):
    cast_src, o_ref, cast_dst, mn_ref = (rest[:n_casts], rest[n_casts],
                                         rest[n_casts + 1:-1], rest[-1])
    _run_casts(cast_src, cast_dst)

    @pl.when(pl.program_id(1) == 0)
    def _():
        mn_ref[...] = _rms(mem_ref[...], g_ref[0]).astype(BF16)

    o_ref[0] = _dot(mn_ref[...], w_ref[0].astype(BF16)).astype(BF16)


def _mem_kv(mem, n_mem, w_kv, casts=()):
    b, m, d = mem.shape
    depth = w_kv.shape[0]
    n_col = 2 * d // KV_COLS
    c_args, c_in, c_out, c_shapes = _cast_jobs(casts, depth * n_col, lambda l, j: l * n_col + j)
    outs = pl.pallas_call(
        functools.partial(_kv_kernel, n_casts=len(casts)),
        grid=(depth, n_col),
        in_specs=[
            pl.BlockSpec((b * m, d), lambda l, j: (0, 0), pipeline_mode=pl.Buffered(1)),
            pl.BlockSpec((1, 1, d), lambda l, j: (l, 0, 0)),
            pl.BlockSpec((1, d, KV_COLS), lambda l, j: (l, 0, j)),
        ] + c_in,
        out_specs=[pl.BlockSpec((1, b * m, KV_COLS), lambda l, j: (l, 0, j))] + c_out,
        out_shape=[jax.ShapeDtypeStruct((depth, b * m, 2 * d), BF16)] + c_shapes,
        scratch_shapes=[pltpu.VMEM((b * m, d), BF16)],
        compiler_params=_params(("arbitrary", "arbitrary")),
        name="mem_kv",
    )(mem.reshape(b * m, d), n_mem, w_kv, *c_args)
    return outs[0].reshape(depth, b, m, 2 * d), _cast_results(outs[1:], casts)


def _attend_stages(hn, x1, wq_ref, kv_ref, wo_ref, store):
    q = (_dot(hn, wq_ref[0]) * (X_DH ** -0.5)).astype(BF16)
    yield
    heads = [slice(h * X_DH, (h + 1) * X_DH) for h in range(X_HEADS)]
    scores = [_dot_nt(q[:, cols], kv_ref[0, 0, :, cols]) for cols in heads]
    yield
    probs = []
    for s in scores:
        e = jnp.exp(s - jnp.max(s, axis=-1, keepdims=True))
        probs.append((e * (1.0 / jnp.sum(e, axis=-1, keepdims=True))).astype(BF16))
    yield
    o = jnp.concatenate(
        [_dot(p, kv_ref[0, 0, :, D_MODEL + h * X_DH:D_MODEL + (h + 1) * X_DH]).astype(BF16)
         for h, p in enumerate(probs)], axis=1)
    yield
    store(x1 + _dot(o, wo_ref[0]))
    yield


def _xattn_kernel(x_ref, gx_ref, wq_ref, kv_ref, wo_ref, o_ref):
    x = x_ref[0]
    hn = _rms(x, gx_ref[0]).astype(BF16)

    def store(rows):
        o_ref[0] = rows

    for _ in _attend_stages(hn, x, wq_ref, kv_ref, wo_ref, store):
        pass


def _xattn(x, layer, kv, gx, wq, wo):
    b, s, d = x.shape
    tm = XATTN_TM
    x_spec = pl.BlockSpec((1, tm, d), lambda bi, i: (bi, i, 0))
    return pl.pallas_call(
        _xattn_kernel,
        grid=(b, s // tm),
        in_specs=[
            x_spec,
            _layer_block(gx, layer),
            _layer_block(wq, 0),
            pl.BlockSpec((1, 1, MEM_LEN, 2 * d), lambda bi, i: (layer, bi, 0, 0)),
            _layer_block(wo, 0),
        ],
        out_specs=x_spec,
        out_shape=jax.ShapeDtypeStruct((b, s, d), F32),
        compiler_params=_params(("arbitrary", "arbitrary")),
        name="xattn",
    )(x, gx, wq, kv, wo)


def _pool_windows(x, hprev, seq_start, gmix):
    rows = x.shape[0]
    hp = _rms(x, gmix)
    ext = jnp.concatenate([hprev, hp], axis=0)
    row = lax.broadcasted_iota(jnp.int32, (rows, 1), 0)
    ps = []
    for gi, w in enumerate(POOL_WINDOWS):
        lanes = slice(gi * POOL_GC, (gi + 1) * POOL_GC)
        win = ext[:, lanes]
        k = 1
        while k < w:
            win = win + pltpu.roll(win, k, axis=0)
            k *= 2
        if seq_start is False:
            inv_cnt = 1.0 / w
        else:
            inv_cnt = 1.0 / jnp.where(seq_start, jnp.minimum(row + 1, w), w).astype(F32)
        ps.append(win[POOL_HALO:] * inv_cnt - hp[:, lanes])
    return jnp.concatenate(ps, axis=1), hp[rows - POOL_HALO:]


def _interleave(main, side):
    for _ in main:
        next(side, None)
    for _ in side:
        pass


def _ffn_pool_kernel(x_ref, g_ref, wgu_ref, wd_ref, gmix_ref, pw_ref, pb_ref, ps_ref, *rest,
                     n_casts, n_tiles, tiles_per_seq):
    cast_src, o_ref, cast_dst = rest[:n_casts], rest[n_casts], rest[n_casts + 1:-2]
    y_ref, tail_ref = rest[-2:]
    t = pl.program_id(0)
    tm = x_ref.shape[0]
    _run_casts(cast_src, cast_dst)

    @pl.when(t == 0)
    def _():
        y_ref[...] = jnp.zeros_like(y_ref)
        tail_ref[...] = jnp.zeros_like(tail_ref)

    def pool_pieces():
        gmix = gmix_ref[0]
        seq_start = lax.rem(t + tiles_per_seq - 1, tiles_per_seq) == 0
        hprev = jnp.where(seq_start, 0.0, tail_ref[...])
        for r0 in range(0, tm, PREP_ROWS):
            y = y_ref[r0:r0 + PREP_ROWS, :]
            p, hprev = _pool_windows(y, hprev, seq_start if r0 == 0 else False, gmix)
            p = p.astype(BF16)
            mixed = jnp.concatenate(
                [_dot(p[:, gi * POOL_GC:(gi + 1) * POOL_GC], pw_ref[0, gi]) + pb_ref[0, gi:gi + 1, :]
                 for gi in range(len(POOL_WINDOWS))], axis=1)
            o_ref[r0:r0 + PREP_ROWS, :] = y + mixed * ps_ref[0]
            yield
        tail_ref[...] = hprev

    @pl.when(t < n_tiles)
    def _():
        result = []
        _interleave(_ffn_chunks(x_ref[...], g_ref[0], wgu_ref, wd_ref, result), pool_pieces())
        y_ref[...] = result[0]

    @pl.when(t == n_tiles)
    def _():
        for _ in pool_pieces():
            pass


def _ffn_pool(x, layer, g, w_gu, w_down, pool_layer, gmix, pw, pb, ps, casts=()):
    b, s, d = x.shape
    t = b * s
    tm = FFN_TM
    n_tiles = t // tm
    n_steps = n_tiles + 1
    c_args, c_in, c_out, c_shapes = _cast_jobs(casts, n_steps)
    outs = pl.pallas_call(
        functools.partial(_ffn_pool_kernel, n_casts=len(casts), n_tiles=n_tiles,
                          tiles_per_seq=s // tm),
        grid=(n_steps,),
        in_specs=[
            pl.BlockSpec((tm, d), lambda i: (jnp.minimum(i, n_tiles - 1), 0)),
            _layer_block(g, layer),
            _layer_block(w_gu, 0),
            _layer_block(w_down, 0),
            _layer_block(gmix, layer),
            _layer_block(pw, pool_layer),
            _layer_block(pb, pool_layer),
            _layer_block(ps, pool_layer),
        ] + c_in,
        out_specs=[pl.BlockSpec((tm, d), lambda i: (jnp.maximum(i - 1, 0), 0))] + c_out,
        out_shape=[jax.ShapeDtypeStruct((t, d), F32)] + c_shapes,
        scratch_shapes=[pltpu.VMEM((tm, d), F32), pltpu.VMEM((POOL_HALO, d), F32)],
        compiler_params=_params(("arbitrary",)),
        name="ffn_pool",
    )(x.reshape(t, d), g, w_gu, w_down, gmix, pw, pb, ps, *c_args)
    return outs[0].reshape(b, s, d), _cast_results(outs[1:], casts)


def _ffn_norm_kernel(x_ref, g_ref, wgu_ref, wd_ref, gf_ref, o_ref, y_ref, *, n_tiles):
    t = pl.program_id(0)
    tm = x_ref.shape[0]

    @pl.when(t == 0)
    def _():
        y_ref[...] = jnp.zeros_like(y_ref)

    def norm_pieces():
        for r0 in range(0, tm, PREP_ROWS):
            o_ref[r0:r0 + PREP_ROWS, :] = _rms(y_ref[r0:r0 + PREP_ROWS, :], gf_ref[...])
            yield

    @pl.when(t < n_tiles)
    def _():
        result = []
        _interleave(_ffn_chunks(x_ref[...], g_ref[0], wgu_ref, wd_ref, result), norm_pieces())
        y_ref[...] = result[0]

    @pl.when(t == n_tiles)
    def _():
        for _ in norm_pieces():
            pass


def _ffn_final(x, layer, g, w_gu, w_down, g_final):
    b, s, d = x.shape
    t = b * s
    tm = FFN_TM
    n_tiles = t // tm
    out = pl.pallas_call(
        functools.partial(_ffn_norm_kernel, n_tiles=n_tiles),
        grid=(n_tiles + 1,),
        in_specs=[
            pl.BlockSpec((tm, d), lambda i: (jnp.minimum(i, n_tiles - 1), 0)),
            _layer_block(g, layer),
            _layer_block(w_gu, 0),
            _layer_block(w_down, 0),
            pl.BlockSpec((1, d), lambda i: (0, 0), pipeline_mode=pl.Buffered(1)),
        ],
        out_specs=pl.BlockSpec((tm, d), lambda i: (jnp.maximum(i - 1, 0), 0)),
        out_shape=jax.ShapeDtypeStruct((t, d), F32),
        scratch_shapes=[pltpu.VMEM((tm, d), F32)],
        compiler_params=_params(("arbitrary",)),
        name="ffn_final",
    )(x.reshape(t, d), g, w_gu, w_down, g_final.reshape(1, d))
    return out.reshape(b, s, d)


def _gla_kernel(x_ref, gmix_ref, win_ref, wg1_ref, wg2_ref, bg_ref, gon_ref, wout_ref,
                o_ref, state_ref, *, tm):
    @pl.when(pl.program_id(1) == 0)
    def _():
        state_ref[...] = jnp.zeros_like(state_ref)

    nc = tm // CHUNK
    x = x_ref[0]
    h = _rms(x, gmix_ref[0]).astype(BF16)
    gate_in = jnp.concatenate([_dot(h[:tm // 2], wg1_ref[0]), _dot(h[tm // 2:], wg1_ref[0])], axis=0)
    z = _dot(gate_in.astype(BF16), wg2_ref[0]) + bg_ref[0]
    qk = _dot(h, win_ref[0, :, 0:2 * GLA_KD])
    zs = z * LOG2E
    glog = (jnp.minimum(zs, 0.0) - jnp.log2(1.0 + jnp.exp2(-jnp.abs(zs)))) * (1.0 / GATE_TAU)
    g_hi = glog.astype(BF16)
    g_lo = (glog - g_hi.astype(F32)).astype(BF16)
    row = lax.broadcasted_iota(jnp.int32, (CHUNK, CHUNK), 0)
    col = lax.broadcasted_iota(jnp.int32, (CHUNK, CHUNK), 1)
    causal = row >= col
    tril = causal.astype(BF16)
    tril2 = jnp.concatenate([tril, tril], axis=1)
    chunk_rows = [slice(c * CHUNK, (c + 1) * CHUNK) for c in range(nc)]
    heads = [(slice(hh * GLA_DK, (hh + 1) * GLA_DK), slice(hh * GLA_DV, (hh + 1) * GLA_DV))
             for hh in range(GLA_HEADS)]

    b = jnp.concatenate(
        [_dot(tril2, jnp.concatenate([g_hi[rows], g_lo[rows]], axis=0)) for rows in chunk_rows],
        axis=0)
    b = b.reshape(nc, CHUNK, GLA_KD)
    b_ref = b[:, CHUNK // 2 - 1:CHUNK // 2, :]
    b_last = b[:, CHUNK - 1:CHUNK, :]
    v = _dot(h, win_ref[0, :, 2 * GLA_KD:2 * GLA_KD + GLA_VD]).astype(BF16)
    q = (qk[:, 0:GLA_KD] * (GLA_DK ** -0.5)).reshape(nc, CHUNK, GLA_KD)
    k = qk[:, GLA_KD:].reshape(nc, CHUNK, GLA_KD)
    qa = (q * jnp.exp2(b - b_ref)).astype(BF16)
    qi = (q * jnp.exp2(b)).astype(BF16)
    kb = (k * jnp.exp2(b_ref - b)).astype(BF16)
    kst = k * jnp.exp2(b_last - b)
    r = _dot(h, win_ref[0, :, 2 * GLA_KD + GLA_VD:])
    gate = r * jax.nn.sigmoid(r)

    n_pairs = nc // 2
    pair_rows = [slice(p * 2 * CHUNK, (p + 1) * 2 * CHUNK) for p in range(n_pairs)]

    def even(t):
        return t.reshape((n_pairs, 2) + t.shape[1:])[:, 0]

    def odd(t):
        return t.reshape((n_pairs, 2) + t.shape[1:])[:, 1]

    last_a, last_b = even(b_last), odd(b_last)
    qi_a = even(qi)
    qi_b = odd(qi)
    qid_b = (odd(q) * jnp.exp2(odd(b) + last_a)).astype(BF16)
    kst_a = even(kst).astype(BF16)
    kstd_a = even(k) * jnp.exp2(last_a - even(b) + last_b)
    kst_b = odd(kst)
    kt = [jnp.concatenate([kstd_a[p].T, kst_b[p].T], axis=1).astype(BF16)
          for p in range(n_pairs)]
    tail = even(b)[:, CHUNK - 8:CHUNK, :] + odd(b)[:, CHUNK - 8:CHUNK, :]
    decay = [jnp.exp2(tail[p].T[:, 7:8]) for p in range(n_pairs)]

    def masked(c, kl):
        return jnp.where(causal, _dot_nt(qa[c][:, kl], kb[c][:, kl]), 0.0).astype(BF16)

    zero_blk = jnp.zeros((CHUNK, CHUNK), BF16)
    lhs = [[jnp.concatenate(
        [jnp.concatenate([qi_a[p][:, kl], masked(2 * p, kl), zero_blk], axis=1),
         jnp.concatenate([qid_b[p][:, kl], _dot_nt(qi_b[p][:, kl], kst_a[p][:, kl]).astype(BF16),
                          masked(2 * p + 1, kl)], axis=1)], axis=0)
        for kl, _ in heads] for p in range(n_pairs)]
    upd = [[_dot(kt[p][kl], v[pair_rows[p], vl]) for kl, vl in heads] for p in range(n_pairs)]

    o_rows = [[None] * GLA_HEADS for _ in range(n_pairs)]
    for hh, (kl, vl) in enumerate(heads):
        st = state_ref[hh]
        for p in range(n_pairs):
            rhs = jnp.concatenate([st.astype(BF16), v[pair_rows[p], vl]], axis=0)
            o_rows[p][hh] = _dot(lhs[p][hh], rhs)
            st = decay[p][kl] * st + upd[p][hh]
        state_ref[hh] = st
    o = jnp.concatenate([jnp.concatenate(o_rows[p], axis=1) for p in range(n_pairs)], axis=0)

    gon = gon_ref[0]
    outs = [(_rms(o[:, vl], gon) * gate[:, vl]).astype(BF16) for _, vl in heads]
    o_ref[0] = x + _dot(jnp.concatenate(outs, axis=1), wout_ref[0])


def _gla(x, layer, gla_layer, gmix, w_in, w_g1, w_g2, b_g, g_on, w_out):
    b, s, d = x.shape
    tm = GLA_TM
    x_spec = pl.BlockSpec((1, tm, d), lambda bi, i: (bi, i, 0))
    return pl.pallas_call(
        functools.partial(_gla_kernel, tm=tm),
        grid=(b, s // tm),
        in_specs=[
            x_spec,
            _layer_block(gmix, layer),
            _layer_block(w_in, 0),
            _layer_block(w_g1, gla_layer),
            _layer_block(w_g2, gla_layer),
            _layer_block(b_g, gla_layer),
            _layer_block(g_on, gla_layer),
            _layer_block(w_out, 0),
        ],
        out_specs=x_spec,
        out_shape=jax.ShapeDtypeStruct((b, s, d), F32),
        scratch_shapes=[pltpu.VMEM((GLA_HEADS, GLA_DK, GLA_DV), F32)],
        compiler_params=_params(("arbitrary", "arbitrary")),
        name="gla",
    )(x, gmix, w_in, w_g1, w_g2, b_g, g_on, w_out)


def _rows(p):
    return p.reshape(p.shape[0], 1, p.shape[1])


def kernel(x, mem, n_ffn1, ffn1_w_gu, ffn1_w_down, n_mix, gla_w_in, gla_w_g1, gla_w_g2,
           gla_b_g, gla_onorm, gla_w_out, pool_w, pool_b, pool_scale, n_xattn, n_mem,
           x_w_q, x_w_kv, x_w_o, n_ffn2, ffn2_w_gu, ffn2_w_down, n_final):
    n_ffn1, n_mix, n_xattn, n_mem, n_ffn2, gla_b_g, gla_onorm, pool_scale = (
        _rows(p) for p in (n_ffn1, n_mix, n_xattn, n_mem, n_ffn2, gla_b_g, gla_onorm, pool_scale))
    gla_w_g1, gla_w_g2, pool_w = (w.astype(BF16) for w in (gla_w_g1, gla_w_g2, pool_w))
    kv, first_w = _mem_kv(mem, n_mem, x_w_kv,
                          casts=[(ffn1_w_gu, 0), (ffn1_w_down, 0), (gla_w_in, 0), (gla_w_out, 0)])
    ffn_w, gla_w = first_w[:2], first_w[2:]
    for i in range(DEPTH):
        casts = [(ffn2_w_gu, i), (ffn2_w_down, i), (x_w_q, i), (x_w_o, i)]
        j = i // 2
        if i % 2 == 0:
            x, (f2_gu, f2_down, wq, wo) = _ffn(x, i, n_ffn1, *ffn_w, casts=casts)
            x = _gla(x, i, j, n_mix, gla_w[0], gla_w_g1, gla_w_g2, gla_b_g, gla_onorm, gla_w[1])
        else:
            x, (f2_gu, f2_down, wq, wo) = _ffn_pool(
                x, i, n_ffn1, *ffn_w, j, n_mix, pool_w, pool_b, pool_scale, casts=casts)
        x = _xattn(x, i, kv, n_xattn, wq, wo)
        casts = []
        if i + 1 < DEPTH:
            casts = [(ffn1_w_gu, i + 1), (ffn1_w_down, i + 1)]
            if (i + 1) % 2 == 0:
                casts += [(gla_w_in, (i + 1) // 2), (gla_w_out, (i + 1) // 2)]
        if i == DEPTH - 1:
            return _ffn_final(x, i, n_ffn2, f2_gu, f2_down, n_final)
        x, cast_outs = _ffn(x, i, n_ffn2, f2_gu, f2_down, casts=casts)
        ffn_w, gla_w = cast_outs[:2], cast_outs[2:]
    return x
```
